```python
import math
import jax, jax.numpy as jnp
from jax import lax
import numpy as np

D_MODEL = 4096
BATCH = 1
SEQ = 16384
DEPTH = 4

D_MIX = D_MODEL
RET_WIDTH = D_MIX // 2
S5_WIDTH = D_MIX - RET_WIDTH
RET_HEADS = 8
RET_HEAD_DIM = RET_WIDTH // RET_HEADS
RET_CHUNK = 128
ROPE_BASE = 10000.0
S5_GROUP = 16
S5_GROUPS = S5_WIDTH // S5_GROUP
S5_STATE = 64
S5_CHUNK = 128
D_FF = 5504
IN_COLS = 4 * RET_WIDTH + S5_WIDTH
NORM_EPS = 1e-6

kernel_name = "hymba_retention_s5_macaron"


def rmsnorm(x, g):
    xf = x.astype(jnp.float32)
    y = xf * lax.rsqrt(jnp.mean(xf * xf, axis=-1, keepdims=True) + NORM_EPS)
    return (y * g.astype(jnp.float32)).astype(x.dtype)


def swiglu(x, w1, w3, w2):
    return (jax.nn.silu(x @ w1) * (x @ w3)) @ w2


def rotate_every_two(t):
    t1 = t[..., ::2]
    t2 = t[..., 1::2]
    return jnp.stack((-t2, t1), axis=-1).reshape(t.shape)


def retention(q, k, v, g, gain):
    B, L, _ = q.shape
    nc = L // RET_CHUNK
    angle = 1.0 / (ROPE_BASE ** jnp.linspace(0.0, 1.0, RET_HEAD_DIM // 2, dtype=jnp.float32))
    angle = jnp.repeat(angle, 2)
    ang = jnp.arange(L, dtype=jnp.float32)[:, None] * angle[None, :]
    sin, cos = jnp.sin(ang), jnp.cos(ang)

    def heads(t):
        return t.astype(jnp.float32).reshape(B, L, RET_HEADS, RET_HEAD_DIM).transpose(0, 2, 1, 3)

    qh = heads(q)
    kh = heads(k) * (RET_HEAD_DIM ** -0.5)
    vh = heads(v)
    qh = qh * cos + rotate_every_two(qh) * sin
    kh = kh * cos + rotate_every_two(kh) * sin

    log_gamma = jnp.log(1.0 - 2.0 ** (-5.0 - jnp.arange(RET_HEADS, dtype=jnp.float32)))
    n = jnp.arange(RET_CHUNK, dtype=jnp.float32)
    rel = n[:, None] - n[None, :]
    intra = jnp.where(rel >= 0, jnp.exp(log_gamma[:, None, None] * jnp.maximum(rel, 0.0)), 0.0)
    q_decay = jnp.exp(log_gamma[:, None] * (n + 1.0))[..., None]
    k_decay = jnp.exp(log_gamma[:, None] * (RET_CHUNK - 1.0 - n))[..., None]
    chunk_decay = jnp.exp(log_gamma * RET_CHUNK)[:, None, None]

    def to_chunks(t):
        return jnp.moveaxis(t.reshape(B, RET_HEADS, nc, RET_CHUNK, RET_HEAD_DIM), 2, 0)

    def step(state, qkv):
        qc, kc, vc = qkv
        scores = jnp.einsum('bhnd,bhmd->bhnm', qc, kc) * intra
        inner = jnp.einsum('bhnm,bhme->bhne', scores, vc)
        cross = jnp.einsum('bhnd,bhde->bhne', qc, state) * q_decay
        new_state = state * chunk_decay + jnp.einsum('bhmd,bhme->bhde', kc * k_decay, vc)
        return new_state, inner + cross

    state0 = jnp.zeros((B, RET_HEADS, RET_HEAD_DIM, RET_HEAD_DIM), jnp.float32)
    _, out = lax.scan(step, state0, (to_chunks(qh), to_chunks(kh), to_chunks(vh)))
    out = jnp.moveaxis(out, 0, 2).reshape(B, RET_HEADS, L, RET_HEAD_DIM)
    out = out * lax.rsqrt(jnp.mean(out * out, axis=-1, keepdims=True) + NORM_EPS)
    out = out.transpose(0, 2, 1, 3).reshape(B, L, RET_WIDTH) * gain.astype(jnp.float32)
    return out.astype(g.dtype) * jax.nn.silu(g)


def s5_mixer(u, a_re, a_im, b_re, b_im, c_re, c_im, d_skip, log_step, w_glu, b_glu, gain):
    B, L, _ = u.shape
    nc = L // S5_CHUNK
    f32 = jnp.float32
    lam = lax.complex(a_re.astype(f32), a_im.astype(f32))
    delta = jnp.exp(log_step.astype(f32))[:, None]
    lam_bar = jnp.exp(lam * delta)
    b_mat = lax.complex(b_re.astype(f32), b_im.astype(f32))
    b_bar = ((lam_bar - 1.0) / lam)[..., None] * b_mat
    c_mat = lax.complex(c_re.astype(f32), c_im.astype(f32))
    uf = u.astype(f32)
    uc = jnp.moveaxis(uf.reshape(B, nc, S5_CHUNK, S5_GROUPS, S5_GROUP), 1, 0)

    def binop(e1, e2):
        a1, b1 = e1
        a2, b2 = e2
        return a2 * a1, a2 * b1 + b2

    def step(h_prev, u_c):
        bu = jnp.einsum('gpc,btgc->btgp', b_bar, u_c.astype(jnp.complex64))
        a = jnp.broadcast_to(lam_bar, bu.shape)
        a_cum, h_local = lax.associative_scan(binop, (a, bu), axis=1)
        hs = h_local + a_cum * h_prev[:, None]
        y = jnp.einsum('gcp,btgp->btgc', c_mat, hs).real
        return hs[:, -1], y

    h0 = jnp.zeros((B, S5_GROUPS, S5_STATE), jnp.complex64)
    _, y = lax.scan(step, h0, uc)
    y = jnp.moveaxis(y, 0, 1).reshape(B, L, S5_WIDTH) + d_skip.astype(f32) * uf
    y = y.astype(u.dtype)
    y = y * jax.nn.sigmoid(jax.nn.gelu(y) @ w_glu + b_glu)
    return rmsnorm(y, gain)


def setup_inputs(seed: int = 0) -> dict:
    key = jax.random.key(seed)
    ks = jax.random.split(key, 26)
    f32 = jnp.float32

    def nrm(k, shape, scale):
        return jax.random.normal(k, shape, f32) * scale

    def gain(k, shape):
        return 1.0 + 0.01 * jax.random.normal(k, shape, f32)

    x = jax.random.normal(ks[0], (BATCH, SEQ, D_MODEL), f32)
    n_idx = jnp.arange(S5_STATE, dtype=f32)
    s5_a_re = -0.5 + 0.01 * jax.random.normal(ks[7], (DEPTH, S5_GROUPS, S5_STATE), f32)
    s5_a_im = math.pi * n_idx + 0.01 * jax.random.normal(ks[8], (DEPTH, S5_GROUPS, S5_STATE), f32)
    s5_log_step = jax.random.uniform(ks[15], (DEPTH, S5_GROUPS), f32, math.log(1e-3), math.log(1e-1))
    return {
        "x": x,
        "ffn1_norm": gain(ks[1], (DEPTH, D_MODEL)),
        "ffn1_w1": nrm(ks[2], (DEPTH, D_MODEL, D_FF), D_MODEL ** -0.5),
        "ffn1_w3": nrm(ks[3], (DEPTH, D_MODEL, D_FF), D_MODEL ** -0.5),
        "ffn1_w2": nrm(ks[4], (DEPTH, D_FF, D_MODEL), D_FF ** -0.5),
        "mix_norm": gain(ks[5], (DEPTH, D_MODEL)),
        "w_in": nrm(ks[6], (DEPTH, D_MODEL, IN_COLS), D_MODEL ** -0.5),
        "s5_a_re": s5_a_re,
        "s5_a_im": s5_a_im,
        "s5_b_re": nrm(ks[9], (DEPTH, S5_GROUPS, S5_STATE, S5_GROUP), (2.0 * S5_GROUP) ** -0.5),
        "s5_b_im": nrm(ks[10], (DEPTH, S5_GROUPS, S5_STATE, S5_GROUP), (2.0 * S5_GROUP) ** -0.5),
        "s5_c_re": nrm(ks[11], (DEPTH, S5_GROUPS, S5_GROUP, S5_STATE), S5_STATE ** -0.5),
        "s5_c_im": nrm(ks[12], (DEPTH, S5_GROUPS, S5_GROUP, S5_STATE), S5_STATE ** -0.5),
        "s5_d": nrm(ks[13], (DEPTH, S5_WIDTH), 1.0),
        "s5_log_step": s5_log_step,
        "s5_w_glu": nrm(ks[14], (DEPTH, S5_WIDTH, S5_WIDTH), S5_WIDTH ** -0.5),
        "s5_b_glu": nrm(ks[16], (DEPTH, S5_WIDTH), 0.01),
        "s5_out_norm": gain(ks[17], (DEPTH, S5_WIDTH)),
        "ret_out_norm": gain(ks[18], (DEPTH, RET_WIDTH)),
        "w_out": nrm(ks[19], (DEPTH, D_MIX, D_MODEL), D_MIX ** -0.5),
        "ffn2_norm": gain(ks[20], (DEPTH, D_MODEL)),
        "ffn2_w1": nrm(ks[21], (DEPTH, D_MODEL, D_FF), D_MODEL ** -0.5),
        "ffn2_w3": nrm(ks[22], (DEPTH, D_MODEL, D_FF), D_MODEL ** -0.5),
        "ffn2_w2": nrm(ks[23], (DEPTH, D_FF, D_MODEL), D_FF ** -0.5),
        "final_norm": gain(ks[24], (D_MODEL,)),
    }


def reference(x, ffn1_norm, ffn1_w1, ffn1_w3, ffn1_w2, mix_norm, w_in, s5_a_re, s5_a_im,
              s5_b_re, s5_b_im, s5_c_re, s5_c_im, s5_d, s5_log_step, s5_w_glu, s5_b_glu,
              s5_out_norm, ret_out_norm, w_out, ffn2_norm, ffn2_w1, ffn2_w3, ffn2_w2, final_norm):
    h = x
    for l in range(DEPTH):
        h = h + 0.5 * swiglu(rmsnorm(h, ffn1_norm[l]), ffn1_w1[l], ffn1_w3[l], ffn1_w2[l])
        z = rmsnorm(h, mix_norm[l]) @ w_in[l]
        q, k, v, g, u = jnp.split(z, [RET_WIDTH, 2 * RET_WIDTH, 3 * RET_WIDTH, 4 * RET_WIDTH], axis=-1)
        y_ret = retention(q, k, v, g, ret_out_norm[l])
        y_ssm = s5_mixer(u, s5_a_re[l], s5_a_im[l], s5_b_re[l], s5_b_im[l], s5_c_re[l], s5_c_im[l],
                         s5_d[l], s5_log_step[l], s5_w_glu[l], s5_b_glu[l], s5_out_norm[l])
        h = h + jnp.concatenate([y_ret, y_ssm], axis=-1) @ w_out[l]
        h = h + 0.5 * swiglu(rmsnorm(h, ffn2_norm[l]), ffn2_w1[l], ffn2_w3[l], ffn2_w2[l])
    return rmsnorm(h, final_norm)
```

```python
import functools
import math

import jax
import jax.numpy as jnp
from jax import lax
from jax.experimental import pallas as pl
from jax.experimental.pallas import tpu as pltpu

F32 = jnp.float32
BF16 = jnp.bfloat16

NORM_EPS = 1e-6
RET_HEADS = 8
ROPE_BASE = 10000.0

V7X_SUBLANES = 8
V7X_MXU_DIM = 256
V7X_VMEM_LIMIT_BYTES = 58 * 1024 * 1024


def _divisor_tile(n, want):
    for cand in range(want, 0, -V7X_MXU_DIM):
        if n % cand == 0:
            return cand
    raise ValueError(f"no tile <= {want} divides {n}")


def _tiles(seq, d, n_in):
    return dict(
        tm=_divisor_tile(seq, 512),
        tf=256,
        tn_in=_divisor_tile(n_in, 1024),
        tn_out=_divisor_tile(d, 512),
        ret_chunk=_divisor_tile(seq, 256),
        s5_rows=_divisor_tile(seq, 512),
    )


def _cparams(*sem):
    return pltpu.CompilerParams(dimension_semantics=sem, vmem_limit_bytes=V7X_VMEM_LIMIT_BYTES)


def _rms_scale(x):
    return lax.rsqrt(jnp.mean(x * x, axis=-1, keepdims=True) + NORM_EPS)


def _ffn_kernel(h_ref, g_ref, w13_ref, w2_ref, o_ref, a_s, *, tf, n_split):
    @pl.when(pl.program_id(1) == 0)
    def _():
        h = h_ref[...]
        a_s[...] = (h * _rms_scale(h) * g_ref[...]).astype(BF16)
        o_ref[...] = h

    r = jnp.dot(a_s[...], w13_ref[...], preferred_element_type=F32)
    gate = r[:, :tf]
    p = (gate * jax.nn.sigmoid(gate) * r[:, tf:] * 0.5).astype(BF16)
    d = o_ref.shape[1]
    dn = d // n_split
    for s in range(n_split):
        o_ref[:, s * dn:(s + 1) * dn] += jnp.dot(p, w2_ref[:, s * dn:(s + 1) * dn],
                                                 preferred_element_type=F32)


def _ffn(h, gain, w13, w2, layer, t):
    seq, d = h.shape
    tm, tf = t["tm"], t["tf"]
    nf = w2.shape[1] // tf
    return pl.pallas_call(
        functools.partial(_ffn_kernel, tf=tf, n_split=max(1, d // 1024)),
        out_shape=jax.ShapeDtypeStruct((seq, d), F32),
        grid=(seq // tm, nf),
        in_specs=[
            pl.BlockSpec((tm, d), lambda i, f: (i, 0)),
            pl.BlockSpec((None, 1, d), lambda i, f: (layer, 0, 0)),
            pl.BlockSpec((None, d, 2 * tf), lambda i, f: (layer, 0, f)),
            pl.BlockSpec((None, tf, d), lambda i, f: (layer, f, 0)),
        ],
        out_specs=pl.BlockSpec((tm, d), lambda i, f: (i, 0)),
        scratch_shapes=[pltpu.VMEM((tm, d), BF16)],
        compiler_params=_cparams("parallel", "arbitrary"),
        name="ffn",
    )(h, gain, w13, w2)


def _inproj_kernel(h_ref, g_ref, w_ref, z_ref, a_s):
    @pl.when(pl.program_id(1) == 0)
    def _():
        h = h_ref[...]
        a_s[...] = (h * _rms_scale(h) * g_ref[...]).astype(BF16)

    z_ref[...] = jnp.dot(a_s[...], w_ref[...], preferred_element_type=F32)


def _inproj(h, gain, w_in, layer, t):
    seq, d = h.shape
    n = w_in.shape[-1]
    tm, tn = t["tm"], t["tn_in"]
    return pl.pallas_call(
        _inproj_kernel,
        out_shape=jax.ShapeDtypeStruct((seq, n), F32),
        grid=(seq // tm, n // tn),
        in_specs=[
            pl.BlockSpec((tm, d), lambda i, j: (i, 0)),
            pl.BlockSpec((None, 1, d), lambda i, j: (layer, 0, 0)),
            pl.BlockSpec((None, d, tn), lambda i, j: (layer, 0, j)),
        ],
        out_specs=pl.BlockSpec((tm, tn), lambda i, j: (i, j)),
        scratch_shapes=[pltpu.VMEM((tm, d), BF16)],
        compiler_params=_cparams("parallel", "arbitrary"),
        name="inproj",
    )(h, gain, w_in)


def _ret_kernel(q_ref, k_ref, v_ref, g_ref, cos_ref, sin_ref, dm_ref, qd_ref, kd_ref, cd_ref, gain_ref,
                o_ref, s_s, *, k_scale):
    @pl.when(pl.program_id(1) == 0)
    def _():
        s_s[...] = jnp.zeros_like(s_s)

    half = cos_ref.shape[1]
    cos = cos_ref[...]
    sin = sin_ref[...]

    def rope(t):
        te, to = t[:, :half], t[:, half:]
        return jnp.concatenate([te * cos - to * sin, to * cos + te * sin], axis=1)

    qr = rope(q_ref[...])
    kr = rope(k_ref[...]) * k_scale
    qb = qr.astype(BF16)
    kb = kr.astype(BF16)
    vb = v_ref[...].astype(BF16)
    scores = lax.dot_general(qb, kb, (((1,), (1,)), ((), ())), preferred_element_type=F32)
    scores = (scores * dm_ref[...]).astype(BF16)
    inner = jnp.dot(scores, vb, preferred_element_type=F32)
    state = s_s[...]
    cross = jnp.dot(qb, state.astype(BF16), preferred_element_type=F32) * qd_ref[...]
    kdec = (kr * kd_ref[...]).astype(BF16)
    s_s[...] = state * cd_ref[...] + lax.dot_general(kdec, vb, (((0,), (0,)), ((), ())),
                                                    preferred_element_type=F32)
    out = inner + cross
    out = out * _rms_scale(out) * gain_ref[...]
    g = g_ref[...]
    o_ref[...] = (out * (g * jax.nn.sigmoid(g))).astype(o_ref.dtype)


def _retention(z, cos, sin, tabs, gain, layer, rw, t):
    seq = z.shape[0]
    hd = rw // RET_HEADS
    c = t["ret_chunk"]
    nh = RET_HEADS
    dm, qd, kd, cd = tabs

    def zcol(part):
        return pl.BlockSpec((c, hd), lambda h, i: (i, part * nh + h))

    return pl.pallas_call(
        functools.partial(_ret_kernel, k_scale=float(hd) ** -0.5),
        out_shape=jax.ShapeDtypeStruct((seq, rw), BF16),
        grid=(nh, seq // c),
        in_specs=[
            zcol(0), zcol(1), zcol(2), zcol(3),
            pl.BlockSpec((c, hd // 2), lambda h, i: (i, 0)),
            pl.BlockSpec((c, hd // 2), lambda h, i: (i, 0)),
            pl.BlockSpec((None, c, c), lambda h, i: (h, 0, 0)),
            pl.BlockSpec((None, c, hd), lambda h, i: (h, 0, 0)),
            pl.BlockSpec((None, c, hd), lambda h, i: (h, 0, 0)),
            pl.BlockSpec((None, hd, hd), lambda h, i: (h, 0, 0)),
            pl.BlockSpec((None, None, 1, hd), lambda h, i: (layer, h, 0, 0)),
        ],
        out_specs=pl.BlockSpec((c, hd), lambda h, i: (i, h)),
        scratch_shapes=[pltpu.VMEM((hd, hd), F32)],
        compiler_params=_cparams("parallel", "arbitrary"),
        name="retention",
    )(z, z, z, z, cos, sin, dm, qd, kd, cd, gain)


def _retention_tables(c, hd):
    lg = jnp.log(1.0 - 2.0 ** (-5.0 - jnp.arange(RET_HEADS, dtype=F32)))
    n = jnp.arange(c, dtype=F32)
    rel = n[:, None] - n[None, :]
    dm = jnp.where(rel >= 0, jnp.exp(lg[:, None, None] * jnp.maximum(rel, 0.0)), 0.0)
    qd = jnp.broadcast_to(jnp.exp(lg[:, None] * (n + 1.0))[..., None], (RET_HEADS, c, hd))
    kd = jnp.broadcast_to(jnp.exp(lg[:, None] * (c - 1.0 - n))[..., None], (RET_HEADS, c, hd))
    cd = jnp.broadcast_to(jnp.exp(lg * c)[:, None, None], (RET_HEADS, hd, hd))
    return dm, qd, kd, cd


def _rope_tables(seq, hd):
    angle = 1.0 / (ROPE_BASE ** jnp.linspace(0.0, 1.0, hd // 2, dtype=F32))
    ang = jnp.arange(seq, dtype=F32)[:, None] * angle[None, :]
    return jnp.cos(ang), jnp.sin(ang)


def _s5_kernel(u_ref, are_ref, aim_ref, ls_ref, wbr_ref, wbi_ref, wcr_ref, wci_ref, d_ref, y_ref,
               wb_s, tab_s, car_s, x_s):
    ns = are_ref.shape[-1]
    rows = x_s.shape[0]
    sub = V7X_SUBLANES

    @pl.when(pl.program_id(1) == 0)
    def _():
        delta = jnp.exp(ls_ref[...])
        ar = are_ref[...]
        ai = aim_ref[...]
        xr = ar * delta
        xi = ai * delta

        def lam_pow(n):
            mag = jnp.exp(n * xr)
            return mag * jnp.cos(n * xi), mag * jnp.sin(n * xi)

        lr, li = lam_pow(1.0)
        nr, ni = lr - 1.0, li
        inv = 1.0 / (ar * ar + ai * ai)
        kr = (nr * ar + ni * ai) * inv
        ki = (ni * ar - nr * ai) * inv
        br = wbr_ref[...]
        bi = wbi_ref[...]
        wb_s[:, :ns] = (kr * br - ki * bi).astype(BF16)
        wb_s[:, ns:] = (kr * bi + ki * br).astype(BF16)
        row = lax.broadcasted_iota(jnp.int32, (sub, ns), 0)
        for lvl, shift in enumerate((1, 2, 4)):
            pr, pi_ = lam_pow(float(shift))
            tab_s[2 * lvl] = jnp.where(row >= shift, pr, 0.0)
            tab_s[2 * lvl + 1] = jnp.where(row >= shift, pi_, 0.0)
        pr, pi_ = lam_pow(row.astype(F32) + 1.0)
        tab_s[6] = pr
        tab_s[7] = pi_
        car_s[...] = jnp.zeros_like(car_s)

    u = u_ref[...]
    x_s[...] = jnp.dot(u.astype(BF16), wb_s[...], preferred_element_type=F32)

    def block(b, carry):
        cr, ci = carry
        r0 = pl.multiple_of(b * sub, sub)
        xr = x_s[pl.ds(r0, sub), :ns]
        xi = x_s[pl.ds(r0, sub), ns:]
        for lvl, shift in enumerate((1, 2, 4)):
            ar = tab_s[2 * lvl]
            ai = tab_s[2 * lvl + 1]
            sr = pltpu.roll(xr, shift, 0)
            si = pltpu.roll(xi, shift, 0)
            xr, xi = xr + (ar * sr - ai * si), xi + (ar * si + ai * sr)
        pr = tab_s[6]
        pi_ = tab_s[7]
        xr, xi = xr + (pr * cr - pi_ * ci), xi + (pr * ci + pi_ * cr)
        x_s[pl.ds(r0, sub), :ns] = xr
        x_s[pl.ds(r0, sub), ns:] = xi
        last = sub - 1
        return (jnp.broadcast_to(xr[last:sub, :], (sub, ns)), jnp.broadcast_to(xi[last:sub, :], (sub, ns)))

    cr, ci = lax.fori_loop(0, rows // sub, block, (car_s[0], car_s[1]))
    car_s[0] = cr
    car_s[1] = ci
    hb = x_s[...].astype(BF16)
    y = (jnp.dot(hb[:, :ns], wcr_ref[...], preferred_element_type=F32)
         - jnp.dot(hb[:, ns:], wci_ref[...], preferred_element_type=F32))
    y_ref[...] = y + d_ref[...] * u


def _s5(z, prm, layer, col0, sw, t):
    seq = z.shape[0]
    are, aim, ls, wbr, wbi, wcr, wci, dsk = prm
    nb = sw // V7X_MXU_DIM
    ns = are.shape[-1]
    w = V7X_MXU_DIM
    rows = t["s5_rows"]

    def per_block(shape):
        return pl.BlockSpec((None, None) + shape, lambda j, i: (layer, j, 0, 0))

    return pl.pallas_call(
        _s5_kernel,
        out_shape=jax.ShapeDtypeStruct((seq, sw), F32),
        grid=(nb, seq // rows),
        in_specs=[
            pl.BlockSpec((rows, w), lambda j, i: (i, col0 + j)),
            per_block((1, ns)), per_block((1, ns)), per_block((1, ns)),
            per_block((w, ns)), per_block((w, ns)),
            per_block((ns, w)), per_block((ns, w)),
            per_block((1, w)),
        ],
        out_specs=pl.BlockSpec((rows, w), lambda j, i: (i, j)),
        scratch_shapes=[
            pltpu.VMEM((w, 2 * ns), BF16),
            pltpu.VMEM((8, V7X_SUBLANES, ns), F32),
            pltpu.VMEM((2, V7X_SUBLANES, ns), F32),
            pltpu.VMEM((rows, 2 * ns), F32),
        ],
        compiler_params=_cparams("parallel", "arbitrary"),
        name="s5",
    )(z, are, aim, ls, wbr, wbi, wcr, wci, dsk)


def _s5_params(a_re, a_im, b_re, b_im, c_re, c_im, d_skip, log_step):
    depth, groups, p = a_re.shape
    ch = b_re.shape[-1]
    gb = V7X_MXU_DIM // ch
    nb = groups // gb
    ns = gb * p
    eye = jnp.eye(gb, dtype=F32)

    def lane(v):
        return v.astype(F32).reshape(depth, nb, 1, ns)

    def bdiag_b(b):
        b5 = b.astype(F32).reshape(depth, nb, gb, p, ch).transpose(0, 1, 2, 4, 3)
        return jnp.einsum("ljgcp,gh->ljgchp", b5, eye).reshape(depth, nb, gb * ch, ns)

    def bdiag_c(c):
        c5 = c.astype(F32).reshape(depth, nb, gb, ch, p).transpose(0, 1, 2, 4, 3)
        return jnp.einsum("ljgpc,gh->ljgphc", c5, eye).reshape(depth, nb, ns, gb * ch).astype(BF16)

    ls = jnp.broadcast_to(log_step.astype(F32)[..., None], (depth, groups, p))
    return (lane(a_re), lane(a_im), lane(ls), bdiag_b(b_re), bdiag_b(b_im), bdiag_c(c_re), bdiag_c(c_im),
            d_skip.astype(F32).reshape(depth, nb, 1, gb * ch))


def _mixout_kernel(yr_ref, ys_ref, wg_ref, bg_ref, gs_ref, h_ref, wo_ref, o_ref, cat_s):
    rw = yr_ref.shape[1]

    @pl.when(pl.program_id(1) == 0)
    def _():
        y = ys_ref[...]
        zg = jnp.dot(jax.nn.gelu(y).astype(BF16), wg_ref[...], preferred_element_type=F32) + bg_ref[...]
        y = y * jax.nn.sigmoid(zg)
        cat_s[:, rw:] = (y * _rms_scale(y) * gs_ref[...]).astype(BF16)
        cat_s[:, :rw] = yr_ref[...]

    o_ref[...] = h_ref[...] + jnp.dot(cat_s[...], wo_ref[...], preferred_element_type=F32)


def _mixout(h, y_ret, y_s5, w_glu, b_glu, g_s5, w_out, layer, t):
    seq, d = h.shape
    rw, sw = y_ret.shape[1], y_s5.shape[1]
    tm, tn = t["tm"], t["tn_out"]
    return pl.pallas_call(
        _mixout_kernel,
        out_shape=jax.ShapeDtypeStruct((seq, d), F32),
        grid=(seq // tm, d // tn),
        in_specs=[
            pl.BlockSpec((tm, rw), lambda i, j: (i, 0)),
            pl.BlockSpec((tm, sw), lambda i, j: (i, 0)),
            pl.BlockSpec((None, sw, sw), lambda i, j: (layer, 0, 0)),
            pl.BlockSpec((None, 1, sw), lambda i, j: (layer, 0, 0)),
            pl.BlockSpec((None, 1, sw), lambda i, j: (layer, 0, 0)),
            pl.BlockSpec((tm, tn), lambda i, j: (i, j)),
            pl.BlockSpec((None, rw + sw, tn), lambda i, j: (layer, 0, j)),
        ],
        out_specs=pl.BlockSpec((tm, tn), lambda i, j: (i, j)),
        scratch_shapes=[pltpu.VMEM((tm, rw + sw), BF16)],
        compiler_params=_cparams("parallel", "arbitrary"),
        name="mixout",
    )(y_ret, y_s5, w_glu, b_glu, g_s5, h, w_out)


def _norm_kernel(h_ref, g_ref, o_ref):
    h = h_ref[...]
    o_ref[...] = h * _rms_scale(h) * g_ref[...]


def _final_norm(h, gain, t):
    seq, d = h.shape
    tm = t["tm"]
    return pl.pallas_call(
        _norm_kernel,
        out_shape=jax.ShapeDtypeStruct((seq, d), F32),
        grid=(seq // tm,),
        in_specs=[pl.BlockSpec((tm, d), lambda i: (i, 0)), pl.BlockSpec((1, d), lambda i: (0, 0))],
        out_specs=pl.BlockSpec((tm, d), lambda i: (i, 0)),
        compiler_params=_cparams("parallel"),
        name="final_norm",
    )(h, gain)


def _prep_ffn(w1, w3, w2, tf):
    depth, d, f = w1.shape
    fp = -(-f // tf) * tf
    nf = fp // tf
    pad = ((0, 0), (0, 0), (0, fp - f))
    w1p = jnp.pad(w1, pad).astype(BF16).reshape(depth, d, nf, tf)
    w3p = jnp.pad(w3, pad).astype(BF16).reshape(depth, d, nf, tf)
    w13 = jnp.concatenate([w1p, w3p], axis=3).reshape(depth, d, nf * 2 * tf)
    w2p = jnp.pad(w2, ((0, 0), (0, fp - f), (0, 0))).astype(BF16)
    return w13, w2p


def _prep_w_in(w_in, rw):
    depth, d, _ = w_in.shape
    hd = rw // RET_HEADS
    qk = w_in[..., :2 * rw].reshape(depth, d, 2 * RET_HEADS, hd // 2, 2)
    qk = jnp.swapaxes(qk, -1, -2).reshape(depth, d, 2 * rw)
    return jnp.concatenate([qk, w_in[..., 2 * rw:]], axis=-1).astype(BF16)


def kernel(x, ffn1_norm, ffn1_w1, ffn1_w3, ffn1_w2, mix_norm, w_in, s5_a_re, s5_a_im, s5_b_re, s5_b_im,
           s5_c_re, s5_c_im, s5_d, s5_log_step, s5_w_glu, s5_b_glu, s5_out_norm, ret_out_norm, w_out,
           ffn2_norm, ffn2_w1, ffn2_w3, ffn2_w2, final_norm):
    batch, seq, d = x.shape
    assert batch == 1, "sequence kernels carry state along the row axis of a single sequence"
    depth = w_in.shape[0]
    sw = s5_d.shape[-1]
    rw = ret_out_norm.shape[-1]
    assert w_in.shape[-1] == 4 * rw + sw and rw % V7X_MXU_DIM == 0
    hd = rw // RET_HEADS
    t = _tiles(seq, d, w_in.shape[-1])

    w13_a, w2_a = _prep_ffn(ffn1_w1, ffn1_w3, ffn1_w2, t["tf"])
    w13_b, w2_b = _prep_ffn(ffn2_w1, ffn2_w3, ffn2_w2, t["tf"])
    w_in_b = _prep_w_in(w_in, rw)
    w_glu_b = s5_w_glu.astype(BF16)
    w_out_b = w_out.astype(BF16)
    s5_prm = _s5_params(s5_a_re, s5_a_im, s5_b_re, s5_b_im, s5_c_re, s5_c_im, s5_d, s5_log_step)
    cos, sin = _rope_tables(seq, hd)
    ret_tabs = _retention_tables(t["ret_chunk"], hd)

    def row3(v):
        return v.astype(F32)[:, None, :]

    g1, gm, g2 = row3(ffn1_norm), row3(mix_norm), row3(ffn2_norm)
    bg, gs = row3(s5_b_glu), row3(s5_out_norm)
    gr = ret_out_norm.astype(F32).reshape(depth, RET_HEADS, 1, hd)

    h = x.reshape(seq, d)
    for l in range(depth):
        h = _ffn(h, g1, w13_a, w2_a, l, t)
        z = _inproj(h, gm, w_in_b, l, t)
        y_ret = _retention(z, cos, sin, ret_tabs, gr, l, rw, t)
        y_s5 = _s5(z, s5_prm, l, 4 * rw // V7X_MXU_DIM, sw, t)
        h = _mixout(h, y_ret, y_s5, w_glu_b, bg, gs, w_out_b, l, t)
        h = _ffn(h, g2, w13_b, w2_b, l, t)
    out = _final_norm(h, final_norm.astype(F32)[None, :], t)
    return out.reshape(batch, seq, d)
```

```python
import functools

import jax
import jax.numpy as jnp
from jax import lax
from jax.experimental import pallas as pl
from jax.experimental.pallas import tpu as pltpu

F32 = jnp.float32
BF16 = jnp.bfloat16

NORM_EPS = 1e-6
RET_HEADS = 8
ROPE_BASE = 10000.0

V7X_SUBLANES = 8
V7X_LANES = 128
V7X_MXU_DIM = 256
V7X_VMEM_LIMIT_BYTES = 58 * 1024 * 1024


def _divisor_tile(n, want):
    for cand in range(want, 0, -V7X_MXU_DIM):
        if n % cand == 0:
            return cand
    raise ValueError(f"no tile <= {want} divides {n}")


def _tiles(seq, d, n_in):
    return dict(
        tm=_divisor_tile(seq, 512),
        tf=256,
        tn_in=_divisor_tile(n_in, 1024),
        tn_out=_divisor_tile(d, 512),
        ret_chunk=_divisor_tile(seq, 256),
        s5_rows=_divisor_tile(seq, 256),
        s5_split=2,
    )


def _cparams(*sem):
    return pltpu.CompilerParams(dimension_semantics=sem, vmem_limit_bytes=V7X_VMEM_LIMIT_BYTES)


def _rms_scale(x):
    return lax.rsqrt(jnp.mean(x * x, axis=-1, keepdims=True) + NORM_EPS)


def _ffn_kernel(h_ref, g_ref, w13_ref, w2_ref, gf_ref, o_ref, a_s, *, tf, n_split, final_norm):
    @pl.when(pl.program_id(1) == 0)
    def _():
        h = h_ref[...]
        a_s[...] = (h * _rms_scale(h) * g_ref[...]).astype(BF16)
        o_ref[...] = h

    r = jnp.dot(a_s[...], w13_ref[...], preferred_element_type=F32)
    gate = r[:, :tf]
    p = (gate * jax.nn.sigmoid(gate) * r[:, tf:] * 0.5).astype(BF16)
    d = o_ref.shape[1]
    dn = d // n_split
    for s in range(n_split):
        o_ref[:, s * dn:(s + 1) * dn] += jnp.dot(p, w2_ref[:, s * dn:(s + 1) * dn],
                                                 preferred_element_type=F32)

    if final_norm:
        @pl.when(pl.program_id(1) == pl.num_programs(1) - 1)
        def _():
            hf = o_ref[...]
            o_ref[...] = hf * _rms_scale(hf) * gf_ref[...]


def _ffn(h, gain, w13, w2, layer, t, final_gain, final_norm=False):
    seq, d = h.shape
    tm, tf = t["tm"], t["tf"]
    nf = w2.shape[1] // tf
    return pl.pallas_call(
        functools.partial(_ffn_kernel, tf=tf, n_split=max(1, d // 1024), final_norm=final_norm),
        out_shape=jax.ShapeDtypeStruct((seq, d), F32),
        grid=(seq // tm, nf),
        in_specs=[
            pl.BlockSpec((tm, d), lambda i, f: (i, 0)),
            pl.BlockSpec((None, 1, d), lambda i, f: (layer, 0, 0)),
            pl.BlockSpec((None, d, 2 * tf), lambda i, f: (layer, 0, f)),
            pl.BlockSpec((None, tf, d), lambda i, f: (layer, f, 0)),
            pl.BlockSpec((1, d), lambda i, f: (0, 0)),
        ],
        out_specs=pl.BlockSpec((tm, d), lambda i, f: (i, 0)),
        scratch_shapes=[pltpu.VMEM((tm, d), BF16)],
        compiler_params=_cparams("parallel", "arbitrary"),
        name="ffn",
    )(h, gain, w13, w2, final_gain)


def _inproj_kernel(h_ref, g_ref, w_ref, z_ref, a_s):
    @pl.when(pl.program_id(1) == 0)
    def _():
        h = h_ref[...]
        a_s[...] = (h * _rms_scale(h) * g_ref[...]).astype(BF16)

    z_ref[...] = jnp.dot(a_s[...], w_ref[...], preferred_element_type=F32)


def _inproj(h, gain, w_in, layer, t):
    seq, d = h.shape
    n = w_in.shape[-1]
    tm, tn = t["tm"], t["tn_in"]
    return pl.pallas_call(
        _inproj_kernel,
        out_shape=jax.ShapeDtypeStruct((seq, n), F32),
        grid=(seq // tm, n // tn),
        in_specs=[
            pl.BlockSpec((tm, d), lambda i, j: (i, 0)),
            pl.BlockSpec((None, 1, d), lambda i, j: (layer, 0, 0)),
            pl.BlockSpec((None, d, tn), lambda i, j: (layer, 0, j)),
        ],
        out_specs=pl.BlockSpec((tm, tn), lambda i, j: (i, j)),
        scratch_shapes=[pltpu.VMEM((tm, d), BF16)],
        compiler_params=_cparams("parallel", "arbitrary"),
        name="inproj",
    )(h, gain, w_in)


def _ret_kernel(q_ref, k_ref, v_ref, g_ref, cos_ref, sin_ref, dm_ref, qd_ref, kd_ref, cd_ref, gain_ref,
                o_ref, s_s, *, k_scale):
    @pl.when(pl.program_id(0) == 0)
    def _():
        s_s[...] = jnp.zeros_like(s_s)

    nh, hd = s_s.shape[0], s_s.shape[1]
    half = hd // 2
    cos = cos_ref[...]
    sin = sin_ref[...]

    def rope(t):
        te, to = t[:, :half], t[:, half:]
        return jnp.concatenate([te * cos - to * sin, to * cos + te * sin], axis=1)

    for h in range(nh):
        cols = slice(h * hd, (h + 1) * hd)
        qr = rope(q_ref[:, cols])
        kr = rope(k_ref[:, cols]) * k_scale
        qb = qr.astype(BF16)
        kb = kr.astype(BF16)
        vb = v_ref[:, cols].astype(BF16)
        scores = lax.dot_general(qb, kb, (((1,), (1,)), ((), ())), preferred_element_type=F32)
        scores = (scores * dm_ref[h]).astype(BF16)
        inner = jnp.dot(scores, vb, preferred_element_type=F32)
        state = s_s[h]
        qd = qd_ref[h]
        kd = kd_ref[h]
        cross = (jnp.dot(qb, state.astype(BF16), preferred_element_type=F32)
                 * jnp.concatenate([qd, qd], axis=1))
        kdec = (kr * jnp.concatenate([kd, kd], axis=1)).astype(BF16)
        s_s[h] = state * cd_ref[h] + lax.dot_general(kdec, vb, (((0,), (0,)), ((), ())),
                                                     preferred_element_type=F32)
        out = inner + cross
        out = out * _rms_scale(out) * gain_ref[:, cols]
        g = g_ref[:, cols]
        o_ref[:, cols] = (out * (g * jax.nn.sigmoid(g))).astype(o_ref.dtype)


def _retention(z, cos, sin, tabs, gain, layer, rw, t):
    seq = z.shape[0]
    nh = RET_HEADS
    hd = rw // nh
    c = t["ret_chunk"]
    dm, qd, kd, cd = tabs

    def zpart(part):
        return pl.BlockSpec((c, rw), lambda i: (i, part))

    def whole(a):
        return pl.BlockSpec(a.shape, lambda i: (0,) * a.ndim)

    return pl.pallas_call(
        functools.partial(_ret_kernel, k_scale=float(hd) ** -0.5),
        out_shape=jax.ShapeDtypeStruct((seq, rw), BF16),
        grid=(seq // c,),
        in_specs=[
            zpart(0), zpart(1), zpart(2), zpart(3),
            pl.BlockSpec((c, hd // 2), lambda i: (i, 0)),
            pl.BlockSpec((c, hd // 2), lambda i: (i, 0)),
            whole(dm), whole(qd), whole(kd),
            pl.BlockSpec(memory_space=pltpu.SMEM),
            pl.BlockSpec((None, 1, rw), lambda i: (layer, 0, 0)),
        ],
        out_specs=pl.BlockSpec((c, rw), lambda i: (i, 0)),
        scratch_shapes=[pltpu.VMEM((nh, hd, hd), F32)],
        compiler_params=_cparams("arbitrary"),
        name="retention",
    )(z, z, z, z, cos, sin, dm, qd, kd, cd, gain)


def _retention_tables(c, hd):
    lg = jnp.log(1.0 - 2.0 ** (-5.0 - jnp.arange(RET_HEADS, dtype=F32)))
    n = jnp.arange(c, dtype=F32)
    rel = n[:, None] - n[None, :]
    dm = jnp.where(rel >= 0, jnp.exp(lg[:, None, None] * jnp.maximum(rel, 0.0)), 0.0)
    qd = jnp.broadcast_to(jnp.exp(lg[:, None] * (n + 1.0))[..., None], (RET_HEADS, c, hd // 2))
    kd = jnp.broadcast_to(jnp.exp(lg[:, None] * (c - 1.0 - n))[..., None], (RET_HEADS, c, hd // 2))
    cd = jnp.exp(lg * c)
    return dm, qd, kd, cd


def _rope_tables(seq, hd):
    angle = 1.0 / (ROPE_BASE ** jnp.linspace(0.0, 1.0, hd // 2, dtype=F32))
    ang = jnp.arange(seq, dtype=F32)[:, None] * angle[None, :]
    return jnp.cos(ang), jnp.sin(ang)


def _s5_prep_kernel(are_ref, aim_ref, ls_ref, are_t, aim_t, ls_t, wbr_ref, wbi_ref, wb_ref, lam_ref):
    ns = are_ref.shape[-1]

    def lam_bar(ar, ai, ls):
        delta = jnp.exp(ls)
        mag = jnp.exp(ar * delta)
        return mag * jnp.cos(ai * delta), mag * jnp.sin(ai * delta)

    ar = are_ref[...]
    ai = aim_ref[...]
    lr, li = lam_bar(ar, ai, ls_ref[...])
    nr, ni = lr - 1.0, li
    inv = 1.0 / (ar * ar + ai * ai)
    kr = (nr * ar + ni * ai) * inv
    ki = (ni * ar - nr * ai) * inv
    br = wbr_ref[...]
    bi = wbi_ref[...]
    wb_ref[:, :ns] = (kr * br - ki * bi).astype(BF16)
    wb_ref[:, ns:] = (kr * bi + ki * br).astype(BF16)
    tr, ti = lam_bar(are_t[...], aim_t[...], ls_t[...])
    lam_ref[0] = tr
    lam_ref[1] = ti


def _s5_prep(prm):
    are, aim, ls, are_t, aim_t, ls_t, wbr, wbi = prm
    depth, nb, w, ns = wbr.shape
    nst = are_t.shape[2]

    def blk(*shape):
        return pl.BlockSpec((None, None) + shape, lambda l, j: (l, j) + (0,) * len(shape))

    return pl.pallas_call(
        _s5_prep_kernel,
        out_shape=(jax.ShapeDtypeStruct((depth, nb, w, 2 * ns), BF16),
                   jax.ShapeDtypeStruct((depth, nb, 2, nst, V7X_LANES), F32)),
        grid=(depth, nb),
        in_specs=[blk(1, ns), blk(1, ns), blk(1, ns),
                  blk(nst, V7X_LANES), blk(nst, V7X_LANES), blk(nst, V7X_LANES),
                  blk(w, ns), blk(w, ns)],
        out_specs=(blk(w, 2 * ns), blk(2, nst, V7X_LANES)),
        compiler_params=_cparams("parallel", "parallel"),
        name="s5_prep",
    )(are, aim, ls, are_t, aim_t, ls_t, wbr, wbi)


def _s5_kernel(u_ref, wb_ref, lam_ref, wcr_ref, wci_ref, d_ref, y_ref, x_s, h_s):
    nbk, w = wb_ref.shape[0], wb_ref.shape[1]
    ns = wcr_ref.shape[1]
    nst = ns // V7X_LANES
    rows = u_ref.shape[0]
    pitch = x_s.shape[1] // nst

    @pl.when(pl.program_id(1) == 0)
    def _():
        h_s[...] = jnp.zeros_like(h_s)

    u = u_ref[...]
    ub = u.astype(BF16)
    for jb in range(nbk):
        bu = jnp.dot(ub[:, jb * w:(jb + 1) * w], wb_ref[jb], preferred_element_type=F32)
        for part in range(2):
            for s in range(nst):
                lo = part * ns + s * V7X_LANES
                x_s[2 * jb + part, s * pitch:s * pitch + rows, :] = bu[:, lo:lo + V7X_LANES]

    lam = [(lam_ref[jb, 0], lam_ref[jb, 1]) for jb in range(nbk)]

    def step(t, carry):
        out = []
        now = pl.ds(t, nst, stride=pitch)
        for jb in range(nbk):
            hr, hi = carry[jb]
            lr, li = lam[jb]
            nr = (lr * hr - li * hi) + x_s[2 * jb, now, :]
            ni = (lr * hi + li * hr) + x_s[2 * jb + 1, now, :]
            x_s[2 * jb, now, :] = nr
            x_s[2 * jb + 1, now, :] = ni
            out.append((nr, ni))
        return tuple(out)

    init = tuple((h_s[2 * jb], h_s[2 * jb + 1]) for jb in range(nbk))
    fin = lax.fori_loop(0, rows, step, init, unroll=8)
    for jb in range(nbk):
        h_s[2 * jb] = fin[jb][0]
        h_s[2 * jb + 1] = fin[jb][1]

    for jb in range(nbk):
        def gather(part):
            return jnp.concatenate(
                [x_s[2 * jb + part, s * pitch:s * pitch + rows, :] for s in range(nst)], axis=1).astype(BF16)

        y = (jnp.dot(gather(0), wcr_ref[jb], preferred_element_type=F32)
             - jnp.dot(gather(1), wci_ref[jb], preferred_element_type=F32))
        cols = slice(jb * w, (jb + 1) * w)
        y_ref[:, cols] = y + d_ref[:, cols] * u[:, cols]


def _s5(z, wb, lam, wcr, wci, dsk, layer, col0, sw, t):
    seq = z.shape[0]
    nb, w, ns2 = wb.shape[1], wb.shape[2], wb.shape[3]
    ns = ns2 // 2
    nst = ns // V7X_LANES
    rows = t["s5_rows"]
    split = t["s5_split"]
    nbk = nb // split
    wk = nbk * w
    assert col0 % wk == 0
    pitch = rows + V7X_SUBLANES

    return pl.pallas_call(
        _s5_kernel,
        out_shape=jax.ShapeDtypeStruct((seq, sw), F32),
        grid=(split, seq // rows),
        in_specs=[
            pl.BlockSpec((rows, wk), lambda j, i: (i, col0 // wk + j)),
            pl.BlockSpec((None, nbk, w, ns2), lambda j, i: (layer, j, 0, 0)),
            pl.BlockSpec((None, nbk, 2, nst, V7X_LANES), lambda j, i: (layer, j, 0, 0, 0)),
            pl.BlockSpec((None, nbk, ns, w), lambda j, i: (layer, j, 0, 0)),
            pl.BlockSpec((None, nbk, ns, w), lambda j, i: (layer, j, 0, 0)),
            pl.BlockSpec((None, 1, wk), lambda j, i: (layer, 0, j)),
        ],
        out_specs=pl.BlockSpec((rows, wk), lambda j, i: (i, j)),
        scratch_shapes=[
            pltpu.VMEM((2 * nbk, nst * pitch, V7X_LANES), F32),
            pltpu.VMEM((2 * nbk, nst, V7X_LANES), F32),
        ],
        compiler_params=_cparams("parallel", "arbitrary"),
        name="s5",
    )(z, wb, lam, wcr, wci, dsk)


def _s5_layout(a_re, a_im, b_re, b_im, c_re, c_im, log_step):
    depth, groups, p = a_re.shape
    ch = b_re.shape[-1]
    gb = V7X_MXU_DIM // ch
    nb = groups // gb
    ns = gb * p
    eye = jnp.eye(gb, dtype=F32)

    def lane(v):
        return v.astype(F32).reshape(depth, nb, 1, ns)

    def tile(v):
        return v.astype(F32).reshape(depth, nb, ns // V7X_LANES, V7X_LANES)

    def bdiag_b(b):
        b5 = b.astype(F32).reshape(depth, nb, gb, p, ch)
        return jnp.einsum("ljgpc,gh->ljgchp", b5, eye).reshape(depth, nb, gb * ch, ns)

    def bdiag_c(c):
        c5 = c.astype(F32).reshape(depth, nb, gb, ch, p)
        return jnp.einsum("ljgcp,gh->ljgphc", c5, eye).reshape(depth, nb, ns, gb * ch).astype(BF16)

    ls = jnp.broadcast_to(log_step.astype(F32)[..., None], (depth, groups, p))
    prep = (lane(a_re), lane(a_im), lane(ls), tile(a_re), tile(a_im), tile(ls), bdiag_b(b_re), bdiag_b(b_im))
    return prep, bdiag_c(c_re), bdiag_c(c_im)


def _mixout_kernel(yr_ref, ys_ref, wg_ref, bg_ref, gs_ref, h_ref, wo_ref, o_ref, cat_s):
    rw = yr_ref.shape[1]

    @pl.when(pl.program_id(1) == 0)
    def _():
        y = ys_ref[...]
        zg = jnp.dot(jax.nn.gelu(y).astype(BF16), wg_ref[...], preferred_element_type=F32) + bg_ref[...]
        y = y * jax.nn.sigmoid(zg)
        cat_s[:, rw:] = (y * _rms_scale(y) * gs_ref[...]).astype(BF16)
        cat_s[:, :rw] = yr_ref[...]

    o_ref[...] = h_ref[...] + jnp.dot(cat_s[...], wo_ref[...], preferred_element_type=F32)


def _mixout(h, y_ret, y_s5, w_glu, b_glu, g_s5, w_out, layer, t):
    seq, d = h.shape
    rw, sw = y_ret.shape[1], y_s5.shape[1]
    tm, tn = t["tm"], t["tn_out"]
    return pl.pallas_call(
        _mixout_kernel,
        out_shape=jax.ShapeDtypeStruct((seq, d), F32),
        grid=(seq // tm, d // tn),
        in_specs=[
            pl.BlockSpec((tm, rw), lambda i, j: (i, 0)),
            pl.BlockSpec((tm, sw), lambda i, j: (i, 0)),
            pl.BlockSpec((None, sw, sw), lambda i, j: (layer, 0, 0)),
            pl.BlockSpec((None, 1, sw), lambda i, j: (layer, 0, 0)),
            pl.BlockSpec((None, 1, sw), lambda i, j: (layer, 0, 0)),
            pl.BlockSpec((tm, tn), lambda i, j: (i, j)),
            pl.BlockSpec((None, rw + sw, tn), lambda i, j: (layer, 0, j)),
        ],
        out_specs=pl.BlockSpec((tm, tn), lambda i, j: (i, j)),
        scratch_shapes=[pltpu.VMEM((tm, rw + sw), BF16)],
        compiler_params=_cparams("parallel", "arbitrary"),
        name="mixout",
    )(y_ret, y_s5, w_glu, b_glu, g_s5, h, w_out)


def _cast_w13_kernel(w1_ref, w3_ref, o_ref, *, tf, f_valid):
    col = pl.program_id(1) * tf + lax.broadcasted_iota(jnp.int32, (1, tf), 1)
    ok = col < f_valid
    o_ref[:, :tf] = jnp.where(ok, w1_ref[...], 0.0).astype(BF16)
    o_ref[:, tf:] = jnp.where(ok, w3_ref[...], 0.0).astype(BF16)


def _cast_w2_kernel(w2_ref, o_ref, *, tf, f_valid):
    row = pl.program_id(1) * tf + lax.broadcasted_iota(jnp.int32, (tf, 1), 0)
    o_ref[...] = jnp.where(row < f_valid, w2_ref[...], 0.0).astype(BF16)


def _prep_ffn(w1, w3, w2, tf):
    depth, d, f = w1.shape
    nf = pl.cdiv(f, tf)
    w13 = pl.pallas_call(
        functools.partial(_cast_w13_kernel, tf=tf, f_valid=f),
        out_shape=jax.ShapeDtypeStruct((depth, d, nf * 2 * tf), BF16),
        grid=(depth, nf),
        in_specs=[pl.BlockSpec((None, d, tf), lambda l, j: (l, 0, j)),
                  pl.BlockSpec((None, d, tf), lambda l, j: (l, 0, j))],
        out_specs=pl.BlockSpec((None, d, 2 * tf), lambda l, j: (l, 0, j)),
        compiler_params=_cparams("parallel", "parallel"),
        name="cast_w13",
    )(w1, w3)
    w2p = pl.pallas_call(
        functools.partial(_cast_w2_kernel, tf=tf, f_valid=f),
        out_shape=jax.ShapeDtypeStruct((depth, nf * tf, d), BF16),
        grid=(depth, nf),
        in_specs=[pl.BlockSpec((None, tf, d), lambda l, j: (l, j, 0))],
        out_specs=pl.BlockSpec((None, tf, d), lambda l, j: (l, j, 0)),
        compiler_params=_cparams("parallel", "parallel"),
        name="cast_w2",
    )(w2)
    return w13, w2p


def _cast_w_in_kernel(w_ref, p_ref, o_ref, *, n_perm):
    j = pl.program_id(1)

    @pl.when(j < n_perm)
    def _():
        o_ref[...] = jnp.dot(w_ref[...].astype(BF16), p_ref[...], preferred_element_type=F32).astype(BF16)

    @pl.when(j >= n_perm)
    def _():
        o_ref[...] = w_ref[...].astype(BF16)


def _prep_w_in(w_in, rw):
    depth, d, n = w_in.shape
    hd = rw // RET_HEADS
    src = jnp.arange(hd)
    dst = (src % 2) * (hd // 2) + src // 2
    perm = (dst[:, None] == jnp.arange(hd)[None, :]).astype(BF16)
    return pl.pallas_call(
        functools.partial(_cast_w_in_kernel, n_perm=2 * RET_HEADS),
        out_shape=jax.ShapeDtypeStruct((depth, d, n), BF16),
        grid=(depth, n // hd),
        in_specs=[pl.BlockSpec((None, d, hd), lambda l, j: (l, 0, j)),
                  pl.BlockSpec((hd, hd), lambda l, j: (0, 0))],
        out_specs=pl.BlockSpec((None, d, hd), lambda l, j: (l, 0, j)),
        compiler_params=_cparams("parallel", "parallel"),
        name="cast_w_in",
    )(w_in, perm)


def kernel(x, ffn1_norm, ffn1_w1, ffn1_w3, ffn1_w2, mix_norm, w_in, s5_a_re, s5_a_im, s5_b_re, s5_b_im,
           s5_c_re, s5_c_im, s5_d, s5_log_step, s5_w_glu, s5_b_glu, s5_out_norm, ret_out_norm, w_out,
           ffn2_norm, ffn2_w1, ffn2_w3, ffn2_w2, final_norm):
    batch, seq, d = x.shape
    assert batch == 1, "sequence kernels carry state along the row axis of a single sequence"
    depth = w_in.shape[0]
    sw = s5_d.shape[-1]
    rw = ret_out_norm.shape[-1]
    assert w_in.shape[-1] == 4 * rw + sw and rw % V7X_MXU_DIM == 0
    hd = rw // RET_HEADS
    t = _tiles(seq, d, w_in.shape[-1])

    w13_a, w2_a = _prep_ffn(ffn1_w1, ffn1_w3, ffn1_w2, t["tf"])
    w13_b, w2_b = _prep_ffn(ffn2_w1, ffn2_w3, ffn2_w2, t["tf"])
    w_in_b = _prep_w_in(w_in, rw)
    w_glu_b = s5_w_glu.astype(BF16)
    w_out_b = w_out.astype(BF16)
    s5_raw, wcr, wci = _s5_layout(s5_a_re, s5_a_im, s5_b_re, s5_b_im, s5_c_re, s5_c_im, s5_log_step)
    wb, lam = _s5_prep(s5_raw)
    cos, sin = _rope_tables(seq, hd)
    ret_tabs = _retention_tables(t["ret_chunk"], hd)

    def row3(v):
        return v.astype(F32)[:, None, :]

    g1, gm, g2 = row3(ffn1_norm), row3(mix_norm), row3(ffn2_norm)
    bg, gs, gr, dsk = row3(s5_b_glu), row3(s5_out_norm), row3(ret_out_norm), row3(s5_d)
    gf = final_norm.astype(F32)[None, :]

    h = x.reshape(seq, d)
    for l in range(depth):
        h = _ffn(h, g1, w13_a, w2_a, l, t, gf)
        z = _inproj(h, gm, w_in_b, l, t)
        y_ret = _retention(z, cos, sin, ret_tabs, gr, l, rw, t)
        y_s5 = _s5(z, wb, lam, wcr, wci, dsk, l, 4 * rw, sw, t)
        h = _mixout(h, y_ret, y_s5, w_glu_b, bg, gs, w_out_b, l, t)
        h = _ffn(h, g2, w13_b, w2_b, l, t, gf, final_norm=(l == depth - 1))
    return h.reshape(batch, seq, d)
```

```python
import functools
import math

import jax
import jax.numpy as jnp
from jax import lax
from jax.experimental import pallas as pl
from jax.experimental.pallas import tpu as pltpu

F32 = jnp.float32
BF16 = jnp.bfloat16

NORM_EPS = 1e-6
RET_HEADS = 8
ROPE_BASE = 10000.0

V7X_SUBLANES = 8
V7X_LANES = 128
V7X_MXU_DIM = 256
V7X_VMEM_LIMIT_BYTES = 62 * 1024 * 1024


def _divisor_tile(n, want):
    for cand in range(want, 0, -V7X_MXU_DIM):
        if n % cand == 0:
            return cand
    raise ValueError(f"no tile <= {want} divides {n}")


def _tiles(seq, d, n_in):
    return dict(
        tm=_divisor_tile(seq, 512),
        tf=512,
        tm_proj=_divisor_tile(seq, 1024),
        tn_in=_divisor_tile(n_in, 512),
        tn_out=_divisor_tile(d, 256),
        ret_chunk=_divisor_tile(seq, 256),
        s5_rows=_divisor_tile(seq, 256),
        s5_group=4,
    )


NORM_SLAB_ROWS = 256


def _cparams(*sem):
    return pltpu.CompilerParams(dimension_semantics=sem, vmem_limit_bytes=V7X_VMEM_LIMIT_BYTES)


def _rms_scale(x):
    return lax.rsqrt(jnp.mean(x * x, axis=-1, keepdims=True) + NORM_EPS)


def _slabs(rows):
    step = min(NORM_SLAB_ROWS, rows)
    return [slice(r, r + step) for r in range(0, rows, step)]


def _resident(shape, index_map):
    return pl.BlockSpec(shape, index_map, pipeline_mode=pl.Buffered(1))


def _ffn_kernel(h_ref, g_ref, w13_ref, w2_ref, gf_ref, o_ref, a_s, *, tf, n_split, final_norm):
    @pl.when(pl.program_id(1) == 0)
    def _():
        for rows in _slabs(h_ref.shape[0]):
            h = h_ref[rows, :]
            a_s[rows, :] = (h * _rms_scale(h) * g_ref[...]).astype(BF16)
            o_ref[rows, :] = h

    r = jnp.dot(a_s[...], w13_ref[...], preferred_element_type=F32)
    gate = r[:, :tf]
    p = (gate * jax.nn.sigmoid(gate) * r[:, tf:] * 0.5).astype(BF16)
    d = o_ref.shape[1]
    dn = d // n_split
    for s in range(n_split):
        o_ref[:, s * dn:(s + 1) * dn] += jnp.dot(p, w2_ref[:, s * dn:(s + 1) * dn],
                                                 preferred_element_type=F32)

    if final_norm:
        @pl.when(pl.program_id(1) == pl.num_programs(1) - 1)
        def _():
            for rows in _slabs(o_ref.shape[0]):
                hf = o_ref[rows, :]
                o_ref[rows, :] = hf * _rms_scale(hf) * gf_ref[...]


def _ffn(h, gain, w13, w2, layer, t, final_gain, final_norm=False):
    seq, d = h.shape
    tm, tf = t["tm"], t["tf"]
    nf = w13.shape[1]
    return pl.pallas_call(
        functools.partial(_ffn_kernel, tf=tf, n_split=max(1, d // 1024), final_norm=final_norm),
        out_shape=jax.ShapeDtypeStruct((seq, d), F32),
        grid=(seq // tm, nf),
        in_specs=[
            pl.BlockSpec((tm, d), lambda i, f: (i, 0), pipeline_mode=pl.Buffered(1)),
            pl.BlockSpec((None, 1, d), lambda i, f: (layer, 0, 0)),
            pl.BlockSpec((None, None, d, 2 * tf), lambda i, f: (layer, f, 0, 0)),
            pl.BlockSpec((None, tf, d), lambda i, f: (layer, f, 0)),
            pl.BlockSpec((1, d), lambda i, f: (0, 0)),
        ],
        out_specs=pl.BlockSpec((tm, d), lambda i, f: (i, 0)),
        scratch_shapes=[pltpu.VMEM((tm, d), BF16)],
        compiler_params=_cparams("parallel", "arbitrary"),
        name="ffn",
    )(h, gain, w13, w2, final_gain)


def _inproj_kernel(h_ref, g_ref, w_ref, z_ref, a_s):
    @pl.when(pl.program_id(1) == 0)
    def _():
        for rows in _slabs(h_ref.shape[0]):
            h = h_ref[rows, :]
            a_s[rows, :] = (h * _rms_scale(h) * g_ref[...]).astype(BF16)

    z_ref[...] = jnp.dot(a_s[...], w_ref[...], preferred_element_type=F32)


def _inproj(h, gain, w_in, layer, t):
    seq, d = h.shape
    nblk, tn = w_in.shape[1], w_in.shape[3]
    n = nblk * tn
    tm = t["tm_proj"]
    return pl.pallas_call(
        _inproj_kernel,
        out_shape=jax.ShapeDtypeStruct((seq, n), F32),
        grid=(seq // tm, nblk),
        in_specs=[
            pl.BlockSpec((tm, d), lambda i, j: (i, 0)),
            pl.BlockSpec((None, 1, d), lambda i, j: (layer, 0, 0)),
            pl.BlockSpec((None, None, d, tn), lambda i, j: (layer, j, 0, 0)),
        ],
        out_specs=pl.BlockSpec((tm, tn), lambda i, j: (i, j)),
        scratch_shapes=[pltpu.VMEM((tm, d), BF16)],
        compiler_params=_cparams("parallel", "arbitrary"),
        name="inproj",
    )(h, gain, w_in)


def _ret_kernel(q_ref, k_ref, v_ref, g_ref, cos_ref, sin_ref, dm_ref, qd_ref, kd_ref, cd_ref, gain_ref,
                o_ref, s_s, *, k_scale):
    @pl.when(pl.program_id(0) == 0)
    def _():
        s_s[...] = jnp.zeros_like(s_s)

    nh, hd = s_s.shape[0], s_s.shape[1]
    half = hd // 2
    cos = cos_ref[...]
    sin = sin_ref[...]

    def rope(t):
        te, to = t[:, :half], t[:, half:]
        return jnp.concatenate([te * cos - to * sin, to * cos + te * sin], axis=1)

    for h in range(nh):
        cols = slice(h * hd, (h + 1) * hd)
        qr = rope(q_ref[:, cols])
        kr = rope(k_ref[:, cols]) * k_scale
        qb = qr.astype(BF16)
        kb = kr.astype(BF16)
        vb = v_ref[:, cols].astype(BF16)
        scores = lax.dot_general(qb, kb, (((1,), (1,)), ((), ())), preferred_element_type=F32)
        scores = (scores * dm_ref[h]).astype(BF16)
        inner = jnp.dot(scores, vb, preferred_element_type=F32)
        state = s_s[h]
        qd = qd_ref[h]
        kd = kd_ref[h]
        cross = (jnp.dot(qb, state.astype(BF16), preferred_element_type=F32)
                 * jnp.concatenate([qd, qd], axis=1))
        kdec = (kr * jnp.concatenate([kd, kd], axis=1)).astype(BF16)
        s_s[h] = state * cd_ref[h] + lax.dot_general(kdec, vb, (((0,), (0,)), ((), ())),
                                                     preferred_element_type=F32)
        out = inner + cross
        out = out * _rms_scale(out) * gain_ref[:, cols]
        g = g_ref[:, cols]
        o_ref[:, cols] = (out * (g * jax.nn.sigmoid(g))).astype(o_ref.dtype)


def _retention(z, cos, sin, tabs, gain, layer, rw, t):
    seq = z.shape[0]
    nh = RET_HEADS
    hd = rw // nh
    c = t["ret_chunk"]
    dm, qd, kd, cd = tabs

    def zpart(part):
        return pl.BlockSpec((c, rw), lambda i: (i, part))

    def whole(a):
        return pl.BlockSpec(a.shape, lambda i: (0,) * a.ndim)

    return pl.pallas_call(
        functools.partial(_ret_kernel, k_scale=float(hd) ** -0.5),
        out_shape=jax.ShapeDtypeStruct((seq, rw), BF16),
        grid=(seq // c,),
        in_specs=[
            zpart(0), zpart(1), zpart(2), zpart(3),
            pl.BlockSpec((c, hd // 2), lambda i: (i, 0)),
            pl.BlockSpec((c, hd // 2), lambda i: (i, 0)),
            whole(dm), whole(qd), whole(kd),
            pl.BlockSpec(memory_space=pltpu.SMEM),
            pl.BlockSpec((None, 1, rw), lambda i: (layer, 0, 0)),
        ],
        out_specs=pl.BlockSpec((c, rw), lambda i: (i, 0)),
        scratch_shapes=[pltpu.VMEM((nh, hd, hd), F32)],
        compiler_params=_cparams("arbitrary"),
        name="retention",
    )(z, z, z, z, cos, sin, dm, qd, kd, cd, gain)


def _retention_tables(c, hd):
    lg = jnp.log(1.0 - 2.0 ** (-5.0 - jnp.arange(RET_HEADS, dtype=F32)))
    n = jnp.arange(c, dtype=F32)
    rel = n[:, None] - n[None, :]
    dm = jnp.where(rel >= 0, jnp.exp(lg[:, None, None] * jnp.maximum(rel, 0.0)), 0.0)
    qd = jnp.broadcast_to(jnp.exp(lg[:, None] * (n + 1.0))[..., None], (RET_HEADS, c, hd // 2))
    kd = jnp.broadcast_to(jnp.exp(lg[:, None] * (c - 1.0 - n))[..., None], (RET_HEADS, c, hd // 2))
    cd = jnp.exp(lg * c)
    return dm, qd, kd, cd


def _rope_tables(seq, hd):
    angle = 1.0 / (ROPE_BASE ** jnp.linspace(0.0, 1.0, hd // 2, dtype=F32))
    ang = jnp.arange(seq, dtype=F32)[:, None] * angle[None, :]
    return jnp.cos(ang), jnp.sin(ang)


def _s5_prep_kernel(are_ref, aim_ref, ls_ref, are_t, aim_t, ls_t, wbr_ref, wbi_ref, wb_ref, lam_ref):
    ns = are_ref.shape[-1]

    def lam_bar(ar, ai, ls):
        delta = jnp.exp(ls)
        mag = jnp.exp(ar * delta)
        return mag * jnp.cos(ai * delta), mag * jnp.sin(ai * delta)

    ar = are_ref[...]
    ai = aim_ref[...]
    lr, li = lam_bar(ar, ai, ls_ref[...])
    nr, ni = lr - 1.0, li
    inv = 1.0 / (ar * ar + ai * ai)
    kr = (nr * ar + ni * ai) * inv
    ki = (ni * ar - nr * ai) * inv
    br = wbr_ref[...]
    bi = wbi_ref[...]
    wb_ref[:, :ns] = (kr * br - ki * bi).astype(BF16)
    wb_ref[:, ns:] = (kr * bi + ki * br).astype(BF16)
    tr, ti = lam_bar(are_t[...], aim_t[...], ls_t[...])
    lam_ref[0] = tr
    lam_ref[1] = ti


def _s5_prep(prm):
    are, aim, ls, are_t, aim_t, ls_t, wbr, wbi = prm
    depth, nb, w, ns = wbr.shape
    nst = are_t.shape[2]

    def blk(*shape):
        return pl.BlockSpec((None, None) + shape, lambda l, j: (l, j) + (0,) * len(shape))

    return pl.pallas_call(
        _s5_prep_kernel,
        out_shape=(jax.ShapeDtypeStruct((depth, nb, w, 2 * ns), BF16),
                   jax.ShapeDtypeStruct((depth, nb, 2, nst, V7X_LANES), F32)),
        grid=(depth, nb),
        in_specs=[blk(1, ns), blk(1, ns), blk(1, ns),
                  blk(nst, V7X_LANES), blk(nst, V7X_LANES), blk(nst, V7X_LANES),
                  blk(w, ns), blk(w, ns)],
        out_specs=(blk(w, 2 * ns), blk(2, nst, V7X_LANES)),
        compiler_params=_cparams("parallel", "parallel"),
        name="s5_prep",
    )(are, aim, ls, are_t, aim_t, ls_t, wbr, wbi)


def _s5_kernel(u_ref, wb_ref, lam_ref, wcr_ref, wci_ref, d_ref, y_ref, x_s, hx_s, h_s):
    nb, w = wb_ref.shape[0], wb_ref.shape[1]
    ns = wcr_ref.shape[1]
    nst = ns // V7X_LANES
    rows = u_ref.shape[0]
    pitch = x_s.shape[1] // nst
    group = x_s.shape[0] // 2

    @pl.when(pl.program_id(0) == 0)
    def _():
        h_s[...] = jnp.zeros_like(h_s)

    for first in range(0, nb, group):
        blocks = range(first, first + group)
        for k, jb in enumerate(blocks):
            cols = slice(jb * w, (jb + 1) * w)
            bu = jnp.dot(u_ref[:, cols].astype(BF16), wb_ref[jb], preferred_element_type=F32)
            for part in range(2):
                for s in range(nst):
                    lo = part * ns + s * V7X_LANES
                    x_s[2 * k + part, s * pitch:s * pitch + rows, :] = bu[:, lo:lo + V7X_LANES]

        lam = [(lam_ref[jb, 0], lam_ref[jb, 1]) for jb in blocks]

        def step(t, carry):
            out = []
            now = pl.ds(t, nst, stride=pitch)
            here = pl.ds(pl.multiple_of(t * nst, nst), nst)
            for k in range(group):
                hr, hi = carry[k]
                lr, li = lam[k]
                nr = (lr * hr - li * hi) + x_s[2 * k, now, :]
                ni = (lr * hi + li * hr) + x_s[2 * k + 1, now, :]
                hx_s[2 * k, here, :] = nr
                hx_s[2 * k + 1, here, :] = ni
                out.append((nr, ni))
            return tuple(out)

        init = tuple((h_s[2 * jb], h_s[2 * jb + 1]) for jb in blocks)
        fin = lax.fori_loop(0, rows, step, init, unroll=8)
        for k, jb in enumerate(blocks):
            h_s[2 * jb] = fin[k][0]
            h_s[2 * jb + 1] = fin[k][1]

        for k, jb in enumerate(blocks):
            def states(part):
                return jnp.concatenate(
                    [hx_s[2 * k + part, pl.ds(s, rows, stride=nst), :] for s in range(nst)], axis=1).astype(BF16)

            y = (jnp.dot(states(0), wcr_ref[jb], preferred_element_type=F32)
                 - jnp.dot(states(1), wci_ref[jb], preferred_element_type=F32))
            cols = slice(jb * w, (jb + 1) * w)
            y_ref[:, cols] = y + d_ref[:, cols] * u_ref[:, cols]


def _s5(z, wb, lam, wcr, wci, dsk, layer, col0, sw, t):
    seq = z.shape[0]
    nb, w, ns2 = wb.shape[1], wb.shape[2], wb.shape[3]
    ns = ns2 // 2
    nst = ns // V7X_LANES
    rows = t["s5_rows"]
    assert col0 % sw == 0
    pitch = rows + V7X_SUBLANES
    group = math.gcd(nb, t["s5_group"])

    return pl.pallas_call(
        _s5_kernel,
        out_shape=jax.ShapeDtypeStruct((seq, sw), F32),
        grid=(seq // rows,),
        in_specs=[
            pl.BlockSpec((rows, sw), lambda i: (i, col0 // sw)),
            _resident((None, nb, w, ns2), lambda i: (layer, 0, 0, 0)),
            _resident((None, nb, 2, nst, V7X_LANES), lambda i: (layer, 0, 0, 0, 0)),
            _resident((None, nb, ns, w), lambda i: (layer, 0, 0, 0)),
            _resident((None, nb, ns, w), lambda i: (layer, 0, 0, 0)),
            _resident((None, 1, sw), lambda i: (layer, 0, 0)),
        ],
        out_specs=pl.BlockSpec((rows, sw), lambda i: (i, 0)),
        scratch_shapes=[
            pltpu.VMEM((2 * group, nst * pitch, V7X_LANES), F32),
            pltpu.VMEM((2 * group, nst * rows, V7X_LANES), F32),
            pltpu.VMEM((2 * nb, nst, V7X_LANES), F32),
        ],
        compiler_params=_cparams("arbitrary"),
        name="s5",
    )(z, wb, lam, wcr, wci, dsk)


def _s5_layout(a_re, a_im, b_re, b_im, c_re, c_im, log_step):
    depth, groups, p = a_re.shape
    ch = b_re.shape[-1]
    gb = V7X_MXU_DIM // ch
    nb = groups // gb
    ns = gb * p
    eye = jnp.eye(gb, dtype=F32)

    def lane(v):
        return v.astype(F32).reshape(depth, nb, 1, ns)

    def tile(v):
        return v.astype(F32).reshape(depth, nb, ns // V7X_LANES, V7X_LANES)

    def bdiag_b(b):
        b5 = b.astype(F32).reshape(depth, nb, gb, p, ch)
        return jnp.einsum("ljgpc,gh->ljgchp", b5, eye).reshape(depth, nb, gb * ch, ns)

    def bdiag_c(c):
        c5 = c.astype(F32).reshape(depth, nb, gb, ch, p)
        return jnp.einsum("ljgcp,gh->ljgphc", c5, eye).reshape(depth, nb, ns, gb * ch).astype(BF16)

    ls = jnp.broadcast_to(log_step.astype(F32)[..., None], (depth, groups, p))
    prep = (lane(a_re), lane(a_im), lane(ls), tile(a_re), tile(a_im), tile(ls), bdiag_b(b_re), bdiag_b(b_im))
    return prep, bdiag_c(c_re), bdiag_c(c_im)


def _mixout_kernel(yr_ref, ys_ref, wg_ref, bg_ref, gs_ref, h_ref, wo_ref, o_ref, cat_s):
    rw = yr_ref.shape[1]

    @pl.when(pl.program_id(1) == 0)
    def _():
        for rows in _slabs(ys_ref.shape[0]):
            y = ys_ref[rows, :]
            zg = jnp.dot(jax.nn.gelu(y).astype(BF16), wg_ref[...], preferred_element_type=F32) + bg_ref[...]
            y = y * jax.nn.sigmoid(zg)
            cat_s[rows, rw:] = (y * _rms_scale(y) * gs_ref[...]).astype(BF16)
        cat_s[:, :rw] = yr_ref[...]

    o_ref[...] = h_ref[...] + jnp.dot(cat_s[...], wo_ref[...], preferred_element_type=F32)


def _mixout(h, y_ret, y_s5, w_glu, b_glu, g_s5, w_out, layer, t):
    seq, d = h.shape
    rw, sw = y_ret.shape[1], y_s5.shape[1]
    nblk, tn = w_out.shape[1], w_out.shape[3]
    tm = t["tm_proj"]
    return pl.pallas_call(
        _mixout_kernel,
        out_shape=jax.ShapeDtypeStruct((seq, d), F32),
        grid=(seq // tm, nblk),
        in_specs=[
            pl.BlockSpec((tm, rw), lambda i, j: (i, 0)),
            pl.BlockSpec((tm, sw), lambda i, j: (i, 0)),
            _resident((None, sw, sw), lambda i, j: (layer, 0, 0)),
            pl.BlockSpec((None, 1, sw), lambda i, j: (layer, 0, 0)),
            pl.BlockSpec((None, 1, sw), lambda i, j: (layer, 0, 0)),
            pl.BlockSpec((tm, tn), lambda i, j: (i, j)),
            pl.BlockSpec((None, None, rw + sw, tn), lambda i, j: (layer, j, 0, 0)),
        ],
        out_specs=pl.BlockSpec((tm, tn), lambda i, j: (i, j)),
        scratch_shapes=[pltpu.VMEM((tm, rw + sw), BF16)],
        compiler_params=_cparams("parallel", "arbitrary"),
        name="mixout",
    )(y_ret, y_s5, w_glu, b_glu, g_s5, h, w_out)


def _cast_w13_kernel(w1_ref, w3_ref, o_ref, *, tf, f_valid):
    col = pl.program_id(1) * tf + lax.broadcasted_iota(jnp.int32, (1, tf), 1)
    ok = col < f_valid
    o_ref[:, :tf] = jnp.where(ok, w1_ref[...], 0.0).astype(BF16)
    o_ref[:, tf:] = jnp.where(ok, w3_ref[...], 0.0).astype(BF16)


def _cast_w2_kernel(w2_ref, o_ref, *, tf, f_valid):
    row = pl.program_id(1) * tf + lax.broadcasted_iota(jnp.int32, (tf, 1), 0)
    o_ref[...] = jnp.where(row < f_valid, w2_ref[...], 0.0).astype(BF16)


def _prep_ffn(w1, w3, w2, tf):
    depth, d, f = w1.shape
    nf = pl.cdiv(f, tf)
    w13 = pl.pallas_call(
        functools.partial(_cast_w13_kernel, tf=tf, f_valid=f),
        out_shape=jax.ShapeDtypeStruct((depth, nf, d, 2 * tf), BF16),
        grid=(depth, nf),
        in_specs=[pl.BlockSpec((None, d, tf), lambda l, j: (l, 0, j)),
                  pl.BlockSpec((None, d, tf), lambda l, j: (l, 0, j))],
        out_specs=pl.BlockSpec((None, None, d, 2 * tf), lambda l, j: (l, j, 0, 0)),
        compiler_params=_cparams("parallel", "parallel"),
        name="cast_w13",
    )(w1, w3)
    w2p = pl.pallas_call(
        functools.partial(_cast_w2_kernel, tf=tf, f_valid=f),
        out_shape=jax.ShapeDtypeStruct((depth, nf * tf, d), BF16),
        grid=(depth, nf),
        in_specs=[pl.BlockSpec((None, tf, d), lambda l, j: (l, j, 0))],
        out_specs=pl.BlockSpec((None, tf, d), lambda l, j: (l, j, 0)),
        compiler_params=_cparams("parallel", "parallel"),
        name="cast_w2",
    )(w2)
    return w13, w2p


def _cast_w_in_kernel(w_ref, p_ref, o_ref, *, n_perm):
    j = pl.program_id(1)

    @pl.when(j < n_perm)
    def _():
        o_ref[...] = jnp.dot(w_ref[...].astype(BF16), p_ref[...], preferred_element_type=F32).astype(BF16)

    @pl.when(j >= n_perm)
    def _():
        o_ref[...] = w_ref[...].astype(BF16)


def _prep_w_in(w_in, rw, tn):
    depth, d, n = w_in.shape
    hd = rw // RET_HEADS
    per = tn // hd
    src = jnp.arange(hd)
    dst = (src % 2) * (hd // 2) + src // 2
    perm = (dst[:, None] == jnp.arange(hd)[None, :]).astype(BF16)
    return pl.pallas_call(
        functools.partial(_cast_w_in_kernel, n_perm=2 * RET_HEADS),
        out_shape=jax.ShapeDtypeStruct((depth, n // tn, d, tn), BF16),
        grid=(depth, n // hd),
        in_specs=[pl.BlockSpec((None, d, hd), lambda l, j: (l, 0, j)),
                  pl.BlockSpec((hd, hd), lambda l, j: (0, 0))],
        out_specs=pl.BlockSpec((None, None, d, hd), lambda l, j: (l, j // per, 0, j % per)),
        compiler_params=_cparams("parallel", "parallel"),
        name="cast_w_in",
    )(w_in, perm)


def _cast_block_kernel(w_ref, o_ref):
    o_ref[...] = w_ref[...].astype(BF16)


def _prep_blocked(w, tn):
    depth, k, n = w.shape
    return pl.pallas_call(
        _cast_block_kernel,
        out_shape=jax.ShapeDtypeStruct((depth, n // tn, k, tn), BF16),
        grid=(depth, n // tn),
        in_specs=[pl.BlockSpec((None, k, tn), lambda l, j: (l, 0, j))],
        out_specs=pl.BlockSpec((None, None, k, tn), lambda l, j: (l, j, 0, 0)),
        compiler_params=_cparams("parallel", "parallel"),
        name="cast_blocked",
    )(w)


def kernel(x, ffn1_norm, ffn1_w1, ffn1_w3, ffn1_w2, mix_norm, w_in, s5_a_re, s5_a_im, s5_b_re, s5_b_im,
           s5_c_re, s5_c_im, s5_d, s5_log_step, s5_w_glu, s5_b_glu, s5_out_norm, ret_out_norm, w_out,
           ffn2_norm, ffn2_w1, ffn2_w3, ffn2_w2, final_norm):
    batch, seq, d = x.shape
    assert batch == 1, "sequence kernels carry state along the row axis of a single sequence"
    depth = w_in.shape[0]
    sw = s5_d.shape[-1]
    rw = ret_out_norm.shape[-1]
    assert w_in.shape[-1] == 4 * rw + sw and rw % V7X_MXU_DIM == 0
    hd = rw // RET_HEADS
    t = _tiles(seq, d, w_in.shape[-1])

    w13_a, w2_a = _prep_ffn(ffn1_w1, ffn1_w3, ffn1_w2, t["tf"])
    w13_b, w2_b = _prep_ffn(ffn2_w1, ffn2_w3, ffn2_w2, t["tf"])
    w_in_b = _prep_w_in(w_in, rw, t["tn_in"])
    w_glu_b = s5_w_glu.astype(BF16)
    w_out_b = _prep_blocked(w_out, t["tn_out"])
    s5_raw, wcr, wci = _s5_layout(s5_a_re, s5_a_im, s5_b_re, s5_b_im, s5_c_re, s5_c_im, s5_log_step)
    wb, lam = _s5_prep(s5_raw)
    cos, sin = _rope_tables(seq, hd)
    ret_tabs = _retention_tables(t["ret_chunk"], hd)

    def row3(v):
        return v.astype(F32)[:, None, :]

    g1, gm, g2 = row3(ffn1_norm), row3(mix_norm), row3(ffn2_norm)
    bg, gs, gr, dsk = row3(s5_b_glu), row3(s5_out_norm), row3(ret_out_norm), row3(s5_d)
    gf = final_norm.astype(F32)[None, :]

    h = x.reshape(seq, d)
    for l in range(depth):
        h = _ffn(h, g1, w13_a, w2_a, l, t, gf)
        z = _inproj(h, gm, w_in_b, l, t)
        y_ret = _retention(z, cos, sin, ret_tabs, gr, l, rw, t)
        y_s5 = _s5(z, wb, lam, wcr, wci, dsk, l, 4 * rw, sw, t)
        h = _mixout(h, y_ret, y_s5, w_glu_b, bg, gs, w_out_b, l, t)
        h = _ffn(h, g2, w13_b, w2_b, l, t, gf, final_norm=(l == depth - 1))
    return h.reshape(batch, seq, d)
```

```python
import functools
import math

import jax
import jax.numpy as jnp
from jax import lax
from jax.experimental import pallas as pl
from jax.experimental.pallas import tpu as pltpu

F32 = jnp.float32
BF16 = jnp.bfloat16

NORM_EPS = 1e-6
RET_HEADS = 8
ROPE_BASE = 10000.0

V7X_SUBLANES = 8
V7X_LANES = 128
V7X_MXU_DIM = 256
V7X_VMEM_LIMIT_BYTES = 62 * 1024 * 1024


def _divisor_tile(n, want):
    for cand in range(want, 0, -V7X_MXU_DIM):
        if n % cand == 0:
            return cand
    raise ValueError(f"no tile <= {want} divides {n}")


def _tiles(seq, d, n_in):
    return dict(
        tm=_divisor_tile(seq, 512),
        tf=256,
        tm_proj=_divisor_tile(seq, 1024),
        tn_in=_divisor_tile(n_in, 512),
        tn_out=_divisor_tile(d, 1024),
        ret_chunk=_divisor_tile(seq, 256),
        s5_rows=_divisor_tile(seq, 256),
        s5_group=4,
    )


NORM_SLAB_ROWS = 256


def _cparams(*sem):
    return pltpu.CompilerParams(dimension_semantics=sem, vmem_limit_bytes=V7X_VMEM_LIMIT_BYTES)


def _rms_scale(x):
    return lax.rsqrt(jnp.mean(x * x, axis=-1, keepdims=True) + NORM_EPS)


def _slabs(rows):
    step = min(NORM_SLAB_ROWS, rows)
    return [slice(r, r + step) for r in range(0, rows, step)]


def _resident(shape, index_map):
    return pl.BlockSpec(shape, index_map, pipeline_mode=pl.Buffered(1))


def _ffn_kernel(h_ref, g_ref, w13_ref, w2_ref, w13t_ref, w2t_ref, gf_ref, o_ref, a_s, *, n_split, final_norm):
    d = o_ref.shape[1]
    dn = d // n_split

    def add_chunk(w13, w2):
        r = jnp.dot(a_s[...], w13[...], preferred_element_type=F32)
        width = r.shape[1] // 2
        gate = r[:, :width]
        p = (gate * jax.nn.sigmoid(gate) * r[:, width:] * 0.5).astype(BF16)
        for s in range(n_split):
            cols = slice(s * dn, (s + 1) * dn)
            o_ref[:, cols] += jnp.dot(p, w2[:, cols], preferred_element_type=F32)

    @pl.when(pl.program_id(1) == 0)
    def _():
        for rows in _slabs(h_ref.shape[0]):
            h = h_ref[rows, :]
            a_s[rows, :] = (h * _rms_scale(h) * g_ref[...]).astype(BF16)
            o_ref[rows, :] = h
        add_chunk(w13t_ref, w2t_ref)

    add_chunk(w13_ref, w2_ref)

    if final_norm:
        @pl.when(pl.program_id(1) == pl.num_programs(1) - 1)
        def _():
            for rows in _slabs(o_ref.shape[0]):
                hf = o_ref[rows, :]
                o_ref[rows, :] = hf * _rms_scale(hf) * gf_ref[...]


def _ffn(h, gain, weights, layer, t, final_gain, final_norm=False):
    seq, d = h.shape
    tm = t["tm"]
    w13, w2, w13t, w2t = weights
    nf, tf = w13.shape[1], w2.shape[1] // w13.shape[1]
    tail = w2t.shape[1]
    return pl.pallas_call(
        functools.partial(_ffn_kernel, n_split=max(1, d // 1024), final_norm=final_norm),
        out_shape=jax.ShapeDtypeStruct((seq, d), F32),
        grid=(seq // tm, nf),
        in_specs=[
            pl.BlockSpec((tm, d), lambda i, f: (i, 0)),
            pl.BlockSpec((None, 1, d), lambda i, f: (layer, 0, 0)),
            pl.BlockSpec((None, None, d, 2 * tf), lambda i, f: (layer, f, 0, 0)),
            pl.BlockSpec((None, tf, d), lambda i, f: (layer, f, 0)),
            _resident((None, d, 2 * tail), lambda i, f: (layer, 0, 0)),
            _resident((None, tail, d), lambda i, f: (layer, 0, 0)),
            pl.BlockSpec((1, d), lambda i, f: (0, 0)),
        ],
        out_specs=pl.BlockSpec((tm, d), lambda i, f: (i, 0)),
        scratch_shapes=[pltpu.VMEM((tm, d), BF16)],
        compiler_params=_cparams("parallel", "arbitrary"),
        name="ffn",
    )(h, gain, w13, w2, w13t, w2t, final_gain)


def _inproj_kernel(h_ref, g_ref, w_ref, z_ref, a_s):
    @pl.when(pl.program_id(1) == 0)
    def _():
        for rows in _slabs(h_ref.shape[0]):
            h = h_ref[rows, :]
            a_s[rows, :] = (h * _rms_scale(h) * g_ref[...]).astype(BF16)

    z_ref[...] = jnp.dot(a_s[...], w_ref[...], preferred_element_type=F32)


def _inproj(h, gain, w_in, layer, t):
    seq, d = h.shape
    nblk, tn = w_in.shape[1], w_in.shape[3]
    n = nblk * tn
    tm = t["tm_proj"]
    return pl.pallas_call(
        _inproj_kernel,
        out_shape=jax.ShapeDtypeStruct((seq, n), F32),
        grid=(seq // tm, nblk),
        in_specs=[
            pl.BlockSpec((tm, d), lambda i, j: (i, 0)),
            pl.BlockSpec((None, 1, d), lambda i, j: (layer, 0, 0)),
            pl.BlockSpec((None, None, d, tn), lambda i, j: (layer, j, 0, 0)),
        ],
        out_specs=pl.BlockSpec((tm, tn), lambda i, j: (i, j)),
        scratch_shapes=[pltpu.VMEM((tm, d), BF16)],
        compiler_params=_cparams("parallel", "arbitrary"),
        name="inproj",
    )(h, gain, w_in)


def _ret_kernel(q_ref, k_ref, v_ref, g_ref, cos_ref, sin_ref, dm_ref, qd_ref, kd_ref, cd_ref, gain_ref,
                o_ref, s_s, *, k_scale):
    @pl.when(pl.program_id(0) == 0)
    def _():
        s_s[...] = jnp.zeros_like(s_s)

    nh, hd = s_s.shape[0], s_s.shape[1]
    half = hd // 2
    cos = cos_ref[...]
    sin = sin_ref[...]

    def rope(t):
        te, to = t[:, :half], t[:, half:]
        return jnp.concatenate([te * cos - to * sin, to * cos + te * sin], axis=1)

    for h in range(nh):
        cols = slice(h * hd, (h + 1) * hd)
        qr = rope(q_ref[:, cols])
        kr = rope(k_ref[:, cols]) * k_scale
        qb = qr.astype(BF16)
        kb = kr.astype(BF16)
        vb = v_ref[:, cols].astype(BF16)
        scores = lax.dot_general(qb, kb, (((1,), (1,)), ((), ())), preferred_element_type=F32)
        scores = (scores * dm_ref[h]).astype(BF16)
        inner = jnp.dot(scores, vb, preferred_element_type=F32)
        state = s_s[h]
        qd = qd_ref[h]
        kd = kd_ref[h]
        cross = (jnp.dot(qb, state.astype(BF16), preferred_element_type=F32)
                 * jnp.concatenate([qd, qd], axis=1))
        kdec = (kr * jnp.concatenate([kd, kd], axis=1)).astype(BF16)
        s_s[h] = state * cd_ref[h] + lax.dot_general(kdec, vb, (((0,), (0,)), ((), ())),
                                                     preferred_element_type=F32)
        out = inner + cross
        out = out * _rms_scale(out) * gain_ref[:, cols]
        g = g_ref[:, cols]
        o_ref[:, cols] = (out * (g * jax.nn.sigmoid(g))).astype(o_ref.dtype)


def _retention(z, cos, sin, tabs, gain, layer, rw, t):
    seq = z.shape[0]
    nh = RET_HEADS
    hd = rw // nh
    c = t["ret_chunk"]
    dm, qd, kd, cd = tabs

    def zpart(part):
        return pl.BlockSpec((c, rw), lambda i: (i, part))

    def whole(a):
        return pl.BlockSpec(a.shape, lambda i: (0,) * a.ndim)

    return pl.pallas_call(
        functools.partial(_ret_kernel, k_scale=float(hd) ** -0.5),
        out_shape=jax.ShapeDtypeStruct((seq, rw), BF16),
        grid=(seq // c,),
        in_specs=[
            zpart(0), zpart(1), zpart(2), zpart(3),
            pl.BlockSpec((c, hd // 2), lambda i: (i, 0)),
            pl.BlockSpec((c, hd // 2), lambda i: (i, 0)),
            whole(dm), whole(qd), whole(kd),
            pl.BlockSpec(memory_space=pltpu.SMEM),
            pl.BlockSpec((None, 1, rw), lambda i: (layer, 0, 0)),
        ],
        out_specs=pl.BlockSpec((c, rw), lambda i: (i, 0)),
        scratch_shapes=[pltpu.VMEM((nh, hd, hd), F32)],
        compiler_params=_cparams("arbitrary"),
        name="retention",
    )(z, z, z, z, cos, sin, dm, qd, kd, cd, gain)


def _retention_tables(c, hd):
    lg = jnp.log(1.0 - 2.0 ** (-5.0 - jnp.arange(RET_HEADS, dtype=F32)))
    n = jnp.arange(c, dtype=F32)
    rel = n[:, None] - n[None, :]
    dm = jnp.where(rel >= 0, jnp.exp(lg[:, None, None] * jnp.maximum(rel, 0.0)), 0.0)
    qd = jnp.broadcast_to(jnp.exp(lg[:, None] * (n + 1.0))[..., None], (RET_HEADS, c, hd // 2))
    kd = jnp.broadcast_to(jnp.exp(lg[:, None] * (c - 1.0 - n))[..., None], (RET_HEADS, c, hd // 2))
    cd = jnp.exp(lg * c)
    return dm, qd, kd, cd


def _rope_tables(seq, hd):
    angle = 1.0 / (ROPE_BASE ** jnp.linspace(0.0, 1.0, hd // 2, dtype=F32))
    ang = jnp.arange(seq, dtype=F32)[:, None] * angle[None, :]
    return jnp.cos(ang), jnp.sin(ang)


def _s5_prep_kernel(are_ref, aim_ref, ls_ref, are_t, aim_t, ls_t, wbr_ref, wbi_ref, wb_ref, lam_ref):
    ns = are_ref.shape[-1]

    def lam_bar(ar, ai, ls):
        delta = jnp.exp(ls)
        mag = jnp.exp(ar * delta)
        return mag * jnp.cos(ai * delta), mag * jnp.sin(ai * delta)

    ar = are_ref[...]
    ai = aim_ref[...]
    lr, li = lam_bar(ar, ai, ls_ref[...])
    nr, ni = lr - 1.0, li
    inv = 1.0 / (ar * ar + ai * ai)
    kr = (nr * ar + ni * ai) * inv
    ki = (ni * ar - nr * ai) * inv
    br = wbr_ref[...]
    bi = wbi_ref[...]
    wb_ref[:, :ns] = (kr * br - ki * bi).astype(BF16)
    wb_ref[:, ns:] = (kr * bi + ki * br).astype(BF16)
    tr, ti = lam_bar(are_t[...], aim_t[...], ls_t[...])
    lam_ref[0] = tr
    lam_ref[1] = ti


def _s5_prep(prm):
    are, aim, ls, are_t, aim_t, ls_t, wbr, wbi = prm
    depth, nb, w, ns = wbr.shape
    nst = are_t.shape[2]

    def blk(*shape):
        return pl.BlockSpec((None, None) + shape, lambda l, j: (l, j) + (0,) * len(shape))

    return pl.pallas_call(
        _s5_prep_kernel,
        out_shape=(jax.ShapeDtypeStruct((depth, nb, w, 2 * ns), BF16),
                   jax.ShapeDtypeStruct((depth, nb, 2, nst, V7X_LANES), F32)),
        grid=(depth, nb),
        in_specs=[blk(1, ns), blk(1, ns), blk(1, ns),
                  blk(nst, V7X_LANES), blk(nst, V7X_LANES), blk(nst, V7X_LANES),
                  blk(w, ns), blk(w, ns)],
        out_specs=(blk(w, 2 * ns), blk(2, nst, V7X_LANES)),
        compiler_params=_cparams("parallel", "parallel"),
        name="s5_prep",
    )(are, aim, ls, are_t, aim_t, ls_t, wbr, wbi)


def _s5_kernel(u_ref, wb_ref, lam_ref, wcr_ref, wci_ref, d_ref, wg_ref, bg_ref, gs_ref, y_ref,
               x_s, hx_s, h_s, y_s):
    nb, w = wb_ref.shape[0], wb_ref.shape[1]
    ns = wcr_ref.shape[1]
    nst = ns // V7X_LANES
    rows = u_ref.shape[0]
    pitch = x_s.shape[1] // nst
    group = x_s.shape[0] // 2

    @pl.when(pl.program_id(0) == 0)
    def _():
        h_s[...] = jnp.zeros_like(h_s)

    for first in range(0, nb, group):
        blocks = range(first, first + group)
        for k, jb in enumerate(blocks):
            cols = slice(jb * w, (jb + 1) * w)
            bu = jnp.dot(u_ref[:, cols].astype(BF16), wb_ref[jb], preferred_element_type=F32)
            for part in range(2):
                for s in range(nst):
                    lo = part * ns + s * V7X_LANES
                    x_s[2 * k + part, s * pitch:s * pitch + rows, :] = bu[:, lo:lo + V7X_LANES]

        lam = [(lam_ref[jb, 0], lam_ref[jb, 1]) for jb in blocks]

        def step(t, carry):
            out = []
            now = pl.ds(t, nst, stride=pitch)
            here = pl.ds(pl.multiple_of(t * nst, nst), nst)
            for k in range(group):
                hr, hi = carry[k]
                lr, li = lam[k]
                nr = (lr * hr - li * hi) + x_s[2 * k, now, :]
                ni = (lr * hi + li * hr) + x_s[2 * k + 1, now, :]
                hx_s[2 * k, here, :] = nr
                hx_s[2 * k + 1, here, :] = ni
                out.append((nr, ni))
            return tuple(out)

        init = tuple((h_s[2 * jb], h_s[2 * jb + 1]) for jb in blocks)
        fin = lax.fori_loop(0, rows, step, init, unroll=8)
        for k, jb in enumerate(blocks):
            h_s[2 * jb] = fin[k][0]
            h_s[2 * jb + 1] = fin[k][1]

        for k, jb in enumerate(blocks):
            def states(part):
                return jnp.concatenate(
                    [hx_s[2 * k + part, pl.ds(s, rows, stride=nst), :] for s in range(nst)], axis=1).astype(BF16)

            y = (jnp.dot(states(0), wcr_ref[jb], preferred_element_type=F32)
                 - jnp.dot(states(1), wci_ref[jb], preferred_element_type=F32))
            cols = slice(jb * w, (jb + 1) * w)
            y_s[:, cols] = y + d_ref[:, cols] * u_ref[:, cols]

    y = y_s[...]
    zg = jnp.dot(jax.nn.gelu(y).astype(BF16), wg_ref[...], preferred_element_type=F32) + bg_ref[...]
    y = y * jax.nn.sigmoid(zg)
    y_ref[...] = (y * _rms_scale(y) * gs_ref[...]).astype(y_ref.dtype)


def _s5(z, wb, lam, wcr, wci, dsk, w_glu, b_glu, g_s5, layer, col0, sw, t):
    seq = z.shape[0]
    nb, w, ns2 = wb.shape[1], wb.shape[2], wb.shape[3]
    ns = ns2 // 2
    nst = ns // V7X_LANES
    rows = t["s5_rows"]
    assert col0 % sw == 0
    pitch = rows + V7X_SUBLANES
    group = math.gcd(nb, t["s5_group"])

    return pl.pallas_call(
        _s5_kernel,
        out_shape=jax.ShapeDtypeStruct((seq, sw), BF16),
        grid=(seq // rows,),
        in_specs=[
            pl.BlockSpec((rows, sw), lambda i: (i, col0 // sw)),
            _resident((None, nb, w, ns2), lambda i: (layer, 0, 0, 0)),
            _resident((None, nb, 2, nst, V7X_LANES), lambda i: (layer, 0, 0, 0, 0)),
            _resident((None, nb, ns, w), lambda i: (layer, 0, 0, 0)),
            _resident((None, nb, ns, w), lambda i: (layer, 0, 0, 0)),
            _resident((None, 1, sw), lambda i: (layer, 0, 0)),
            _resident((None, sw, sw), lambda i: (layer, 0, 0)),
            _resident((None, 1, sw), lambda i: (layer, 0, 0)),
            _resident((None, 1, sw), lambda i: (layer, 0, 0)),
        ],
        out_specs=pl.BlockSpec((rows, sw), lambda i: (i, 0)),
        scratch_shapes=[
            pltpu.VMEM((2 * group, nst * pitch, V7X_LANES), F32),
            pltpu.VMEM((2 * group, nst * rows, V7X_LANES), F32),
            pltpu.VMEM((2 * nb, nst, V7X_LANES), F32),
            pltpu.VMEM((rows, sw), F32),
        ],
        compiler_params=_cparams("arbitrary"),
        name="s5",
    )(z, wb, lam, wcr, wci, dsk, w_glu, b_glu, g_s5)


def _s5_layout(a_re, a_im, b_re, b_im, c_re, c_im, log_step):
    depth, groups, p = a_re.shape
    ch = b_re.shape[-1]
    gb = V7X_MXU_DIM // ch
    nb = groups // gb
    ns = gb * p
    eye = jnp.eye(gb, dtype=F32)

    def lane(v):
        return v.astype(F32).reshape(depth, nb, 1, ns)

    def tile(v):
        return v.astype(F32).reshape(depth, nb, ns // V7X_LANES, V7X_LANES)

    def bdiag_b(b):
        b5 = b.astype(F32).reshape(depth, nb, gb, p, ch)
        return jnp.einsum("ljgpc,gh->ljgchp", b5, eye).reshape(depth, nb, gb * ch, ns)

    def bdiag_c(c):
        c5 = c.astype(F32).reshape(depth, nb, gb, ch, p)
        return jnp.einsum("ljgcp,gh->ljgphc", c5, eye).reshape(depth, nb, ns, gb * ch).astype(BF16)

    ls = jnp.broadcast_to(log_step.astype(F32)[..., None], (depth, groups, p))
    prep = (lane(a_re), lane(a_im), lane(ls), tile(a_re), tile(a_im), tile(ls), bdiag_b(b_re), bdiag_b(b_im))
    return prep, bdiag_c(c_re), bdiag_c(c_im)


def _outproj_kernel(yr_ref, ys_ref, h_ref, wo_ref, o_ref):
    rw = yr_ref.shape[1]
    mix = (jnp.dot(yr_ref[...], wo_ref[:rw, :], preferred_element_type=F32)
           + jnp.dot(ys_ref[...], wo_ref[rw:, :], preferred_element_type=F32))
    o_ref[...] = h_ref[...] + mix


def _outproj(h, y_ret, y_ssm, w_out, layer, t):
    seq, d = h.shape
    rw, sw = y_ret.shape[1], y_ssm.shape[1]
    nblk, tn = w_out.shape[1], w_out.shape[3]
    tm = t["tm_proj"]
    return pl.pallas_call(
        _outproj_kernel,
        out_shape=jax.ShapeDtypeStruct((seq, d), F32),
        grid=(seq // tm, nblk),
        in_specs=[
            pl.BlockSpec((tm, rw), lambda i, j: (i, 0)),
            pl.BlockSpec((tm, sw), lambda i, j: (i, 0)),
            pl.BlockSpec((tm, tn), lambda i, j: (i, j)),
            pl.BlockSpec((None, None, rw + sw, tn), lambda i, j: (layer, j, 0, 0)),
        ],
        out_specs=pl.BlockSpec((tm, tn), lambda i, j: (i, j)),
        compiler_params=_cparams("parallel", "arbitrary"),
        name="outproj",
    )(y_ret, y_ssm, h, w_out)


def _cast_w13_kernel(w1_ref, w3_ref, o_ref):
    tf = w1_ref.shape[1]
    o_ref[:, :tf] = w1_ref[...].astype(BF16)
    o_ref[:, tf:] = w3_ref[...].astype(BF16)


def _cast_block_kernel(w_ref, o_ref):
    o_ref[...] = w_ref[...].astype(BF16)


def _prep_ffn(w1, w3, w2, tf):
    depth, d, f = w1.shape
    nf = (f - 1) // tf
    body = nf * tf
    w13 = pl.pallas_call(
        _cast_w13_kernel,
        out_shape=jax.ShapeDtypeStruct((depth, nf, d, 2 * tf), BF16),
        grid=(depth, nf),
        in_specs=[pl.BlockSpec((None, d, tf), lambda l, j: (l, 0, j)),
                  pl.BlockSpec((None, d, tf), lambda l, j: (l, 0, j))],
        out_specs=pl.BlockSpec((None, None, d, 2 * tf), lambda l, j: (l, j, 0, 0)),
        compiler_params=_cparams("parallel", "parallel"),
        name="cast_w13",
    )(w1, w3)
    w2b = pl.pallas_call(
        _cast_block_kernel,
        out_shape=jax.ShapeDtypeStruct((depth, body, d), BF16),
        grid=(depth, nf),
        in_specs=[pl.BlockSpec((None, tf, d), lambda l, j: (l, j, 0))],
        out_specs=pl.BlockSpec((None, tf, d), lambda l, j: (l, j, 0)),
        compiler_params=_cparams("parallel", "parallel"),
        name="cast_w2",
    )(w2)
    pad = -(f - body) % V7X_LANES
    w13t = jnp.concatenate([jnp.pad(w[..., body:], ((0, 0), (0, 0), (0, pad))) for w in (w1, w3)],
                           axis=-1).astype(BF16)
    w2t = jnp.pad(w2[:, body:, :], ((0, 0), (0, pad), (0, 0))).astype(BF16)
    return w13, w2b, w13t, w2t


def _cast_w_in_kernel(w_ref, p_ref, o_ref, *, n_perm):
    j = pl.program_id(1)

    @pl.when(j < n_perm)
    def _():
        o_ref[...] = jnp.dot(w_ref[...].astype(BF16), p_ref[...], preferred_element_type=F32).astype(BF16)

    @pl.when(j >= n_perm)
    def _():
        o_ref[...] = w_ref[...].astype(BF16)


def _prep_w_in(w_in, rw, tn):
    depth, d, n = w_in.shape
    hd = rw // RET_HEADS
    per = tn // hd
    src = jnp.arange(hd)
    dst = (src % 2) * (hd // 2) + src // 2
    perm = (dst[:, None] == jnp.arange(hd)[None, :]).astype(BF16)
    return pl.pallas_call(
        functools.partial(_cast_w_in_kernel, n_perm=2 * RET_HEADS),
        out_shape=jax.ShapeDtypeStruct((depth, n // tn, d, tn), BF16),
        grid=(depth, n // hd),
        in_specs=[pl.BlockSpec((None, d, hd), lambda l, j: (l, 0, j)),
                  pl.BlockSpec((hd, hd), lambda l, j: (0, 0))],
        out_specs=pl.BlockSpec((None, None, d, hd), lambda l, j: (l, j // per, 0, j % per)),
        compiler_params=_cparams("parallel", "parallel"),
        name="cast_w_in",
    )(w_in, perm)


def _prep_blocked(w, tn):
    depth, k, n = w.shape
    return pl.pallas_call(
        _cast_block_kernel,
        out_shape=jax.ShapeDtypeStruct((depth, n // tn, k, tn), BF16),
        grid=(depth, n // tn),
        in_specs=[pl.BlockSpec((None, k, tn), lambda l, j: (l, 0, j))],
        out_specs=pl.BlockSpec((None, None, k, tn), lambda l, j: (l, j, 0, 0)),
        compiler_params=_cparams("parallel", "parallel"),
        name="cast_blocked",
    )(w)


def kernel(x, ffn1_norm, ffn1_w1, ffn1_w3, ffn1_w2, mix_norm, w_in, s5_a_re, s5_a_im, s5_b_re, s5_b_im,
           s5_c_re, s5_c_im, s5_d, s5_log_step, s5_w_glu, s5_b_glu, s5_out_norm, ret_out_norm, w_out,
           ffn2_norm, ffn2_w1, ffn2_w3, ffn2_w2, final_norm):
    batch, seq, d = x.shape
    assert batch == 1, "sequence kernels carry state along the row axis of a single sequence"
    depth = w_in.shape[0]
    sw = s5_d.shape[-1]
    rw = ret_out_norm.shape[-1]
    assert w_in.shape[-1] == 4 * rw + sw and rw % V7X_MXU_DIM == 0
    hd = rw // RET_HEADS
    t = _tiles(seq, d, w_in.shape[-1])

    ffn_a = _prep_ffn(ffn1_w1, ffn1_w3, ffn1_w2, t["tf"])
    ffn_b = _prep_ffn(ffn2_w1, ffn2_w3, ffn2_w2, t["tf"])
    w_in_b = _prep_w_in(w_in, rw, t["tn_in"])
    w_glu_b = s5_w_glu.astype(BF16)
    w_out_b = _prep_blocked(w_out, t["tn_out"])
    s5_raw, wcr, wci = _s5_layout(s5_a_re, s5_a_im, s5_b_re, s5_b_im, s5_c_re, s5_c_im, s5_log_step)
    wb, lam = _s5_prep(s5_raw)
    cos, sin = _rope_tables(seq, hd)
    ret_tabs = _retention_tables(t["ret_chunk"], hd)

    def row3(v):
        return v.astype(F32)[:, None, :]

    g1, gm, g2 = row3(ffn1_norm), row3(mix_norm), row3(ffn2_norm)
    bg, gs, gr, dsk = row3(s5_b_glu), row3(s5_out_norm), row3(ret_out_norm), row3(s5_d)
    gf = final_norm.astype(F32)[None, :]

    h = x.reshape(seq, d)
    for l in range(depth):
        h = _ffn(h, g1, ffn_a, l, t, gf)
        z = _inproj(h, gm, w_in_b, l, t)
        y_ret = _retention(z, cos, sin, ret_tabs, gr, l, rw, t)
        y_ssm = _s5(z, wb, lam, wcr, wci, dsk, w_glu_b, bg, gs, l, 4 * rw, sw, t)
        h = _outproj(h, y_ret, y_ssm, w_out_b, l, t)
        h = _ffn(h, g2, ffn_b, l, t, gf, final_norm=(l == depth - 1))
    return h.reshape(batch, seq, d)
```

```python
import functools
import math

import jax
import jax.numpy as jnp
from jax import lax
from jax.experimental import pallas as pl
from jax.experimental.pallas import tpu as pltpu

F32 = jnp.float32
BF16 = jnp.bfloat16

NORM_EPS = 1e-6
RET_HEADS = 8
ROPE_BASE = 10000.0

V7X_SUBLANES = 8
V7X_LANES = 128
V7X_MXU_DIM = 256
V7X_VMEM_LIMIT_BYTES = 62 * 1024 * 1024


def _divisor_tile(n, want):
    for cand in range(want, 0, -V7X_MXU_DIM):
        if n % cand == 0:
            return cand
    raise ValueError(f"no tile <= {want} divides {n}")


def _tiles(seq, d, n_in):
    return dict(
        tm=_divisor_tile(seq, 512),
        tf=512,
        tm_proj=_divisor_tile(seq, 1024),
        tn_in=_divisor_tile(n_in, 1024),
        tn_out=_divisor_tile(d, 1024),
        ret_chunk=_divisor_tile(seq, 256),
        s5_rows=_divisor_tile(seq, 256),
        s5_group=4,
    )


NORM_SLAB_ROWS = 256


def _cparams(*sem):
    return pltpu.CompilerParams(dimension_semantics=sem, vmem_limit_bytes=V7X_VMEM_LIMIT_BYTES)


def _rms_scale(x):
    return lax.rsqrt(jnp.mean(x * x, axis=-1, keepdims=True) + NORM_EPS)


def _slabs(rows):
    step = min(NORM_SLAB_ROWS, rows)
    return [slice(r, r + step) for r in range(0, rows, step)]


def _resident(shape, index_map):
    return pl.BlockSpec(shape, index_map, pipeline_mode=pl.Buffered(1))


def _row_tile_prefetch(h_hbm, h_s, sem, consume):
    i, j = pl.program_id(0), pl.program_id(1)
    tm = h_s.shape[0]

    def tile_copy(tile):
        return pltpu.make_async_copy(h_hbm.at[pl.ds(pl.multiple_of(tile * tm, tm), tm), :], h_s, sem)

    @pl.when((i == 0) & (j == 0))
    def _():
        tile_copy(0).start()

    @pl.when(j == 0)
    def _():
        tile_copy(i).wait()
        consume()

    @pl.when((j == 1) & (i + 1 < pl.num_programs(0)))
    def _():
        tile_copy(i + 1).start()


def _ffn_kernel(h_hbm, g_ref, w13_ref, w2_ref, gf_ref, o_ref, a_s, h_s, sem, *, n_split, final_norm):
    def start_tile():
        for rows in _slabs(h_s.shape[0]):
            h = h_s[rows, :]
            a_s[rows, :] = (h * _rms_scale(h) * g_ref[...]).astype(BF16)
            o_ref[rows, :] = h

    _row_tile_prefetch(h_hbm, h_s, sem, start_tile)

    r = jnp.dot(a_s[...], w13_ref[...], preferred_element_type=F32)
    tf = r.shape[1] // 2
    gate = r[:, :tf]
    p = (gate * jax.nn.sigmoid(gate) * r[:, tf:] * 0.5).astype(BF16)
    d = o_ref.shape[1]
    dn = d // n_split
    for s in range(n_split):
        cols = slice(s * dn, (s + 1) * dn)
        o_ref[:, cols] += jnp.dot(p, w2_ref[:, cols], preferred_element_type=F32)

    if final_norm:
        @pl.when(pl.program_id(1) == pl.num_programs(1) - 1)
        def _():
            for rows in _slabs(o_ref.shape[0]):
                hf = o_ref[rows, :]
                o_ref[rows, :] = hf * _rms_scale(hf) * gf_ref[...]


def _ffn(h, gain, weights, layer, t, final_gain, final_norm=False):
    seq, d = h.shape
    tm = t["tm"]
    w13, w2 = weights
    nf, tf = w13.shape[1], w13.shape[3] // 2
    assert nf >= 2
    return pl.pallas_call(
        functools.partial(_ffn_kernel, n_split=max(1, d // 1024), final_norm=final_norm),
        out_shape=jax.ShapeDtypeStruct((seq, d), F32),
        grid=(seq // tm, nf),
        in_specs=[
            pl.BlockSpec(memory_space=pl.ANY),
            pl.BlockSpec((None, 1, d), lambda i, f: (layer, 0, 0)),
            pl.BlockSpec((None, None, d, 2 * tf), lambda i, f: (layer, f, 0, 0)),
            pl.BlockSpec((None, tf, d), lambda i, f: (layer, f, 0)),
            pl.BlockSpec((1, d), lambda i, f: (0, 0)),
        ],
        out_specs=pl.BlockSpec((tm, d), lambda i, f: (i, 0)),
        scratch_shapes=[pltpu.VMEM((tm, d), BF16), pltpu.VMEM((tm, d), F32), pltpu.SemaphoreType.DMA(())],
        compiler_params=_cparams("arbitrary", "arbitrary"),
        name="ffn",
    )(h, gain, w13, w2, final_gain)


def _inproj_kernel(h_hbm, g_ref, w_ref, z_ref, a_s, h_s, sem):
    def start_tile():
        for rows in _slabs(h_s.shape[0]):
            h = h_s[rows, :]
            a_s[rows, :] = (h * _rms_scale(h) * g_ref[...]).astype(BF16)

    _row_tile_prefetch(h_hbm, h_s, sem, start_tile)
    z_ref[...] = jnp.dot(a_s[...], w_ref[...], preferred_element_type=F32)


def _inproj(h, gain, w_in, layer, t):
    seq, d = h.shape
    nblk, tn = w_in.shape[1], w_in.shape[3]
    n = nblk * tn
    tm = t["tm_proj"]
    assert nblk >= 2
    return pl.pallas_call(
        _inproj_kernel,
        out_shape=jax.ShapeDtypeStruct((seq, n), F32),
        grid=(seq // tm, nblk),
        in_specs=[
            pl.BlockSpec(memory_space=pl.ANY),
            pl.BlockSpec((None, 1, d), lambda i, j: (layer, 0, 0)),
            pl.BlockSpec((None, None, d, tn), lambda i, j: (layer, j, 0, 0)),
        ],
        out_specs=pl.BlockSpec((tm, tn), lambda i, j: (i, j)),
        scratch_shapes=[pltpu.VMEM((tm, d), BF16), pltpu.VMEM((tm, d), F32), pltpu.SemaphoreType.DMA(())],
        compiler_params=_cparams("arbitrary", "arbitrary"),
        name="inproj",
    )(h, gain, w_in)


def _ret_kernel(q_ref, k_ref, v_ref, g_ref, cos_ref, sin_ref, dm_ref, qd_ref, kd_ref, cd_ref, gain_ref,
                o_ref, s_s, *, k_scale):
    @pl.when(pl.program_id(0) == 0)
    def _():
        s_s[...] = jnp.zeros_like(s_s)

    nh, hd = s_s.shape[0], s_s.shape[1]
    half = hd // 2
    cos = cos_ref[...]
    sin = sin_ref[...]

    def rope(t):
        te, to = t[:, :half], t[:, half:]
        return jnp.concatenate([te * cos - to * sin, to * cos + te * sin], axis=1)

    for h in range(nh):
        cols = slice(h * hd, (h + 1) * hd)
        qr = rope(q_ref[:, cols])
        kr = rope(k_ref[:, cols]) * k_scale
        qb = qr.astype(BF16)
        kb = kr.astype(BF16)
        vb = v_ref[:, cols].astype(BF16)
        scores = lax.dot_general(qb, kb, (((1,), (1,)), ((), ())), preferred_element_type=F32)
        scores = (scores * dm_ref[h]).astype(BF16)
        inner = jnp.dot(scores, vb, preferred_element_type=F32)
        state = s_s[h]
        qd = qd_ref[h]
        kd = kd_ref[h]
        cross = (jnp.dot(qb, state.astype(BF16), preferred_element_type=F32)
                 * jnp.concatenate([qd, qd], axis=1))
        kdec = (kr * jnp.concatenate([kd, kd], axis=1)).astype(BF16)
        s_s[h] = state * cd_ref[h] + lax.dot_general(kdec, vb, (((0,), (0,)), ((), ())),
                                                     preferred_element_type=F32)
        out = inner + cross
        out = out * _rms_scale(out) * gain_ref[:, cols]
        g = g_ref[:, cols]
        o_ref[:, cols] = (out * (g * jax.nn.sigmoid(g))).astype(o_ref.dtype)


def _retention(z, cos, sin, tabs, gain, layer, rw, t):
    seq = z.shape[0]
    nh = RET_HEADS
    hd = rw // nh
    c = t["ret_chunk"]
    dm, qd, kd, cd = tabs

    def zpart(part):
        return pl.BlockSpec((c, rw), lambda i: (i, part))

    def whole(a):
        return pl.BlockSpec(a.shape, lambda i: (0,) * a.ndim)

    return pl.pallas_call(
        functools.partial(_ret_kernel, k_scale=float(hd) ** -0.5),
        out_shape=jax.ShapeDtypeStruct((seq, rw), BF16),
        grid=(seq // c,),
        in_specs=[
            zpart(0), zpart(1), zpart(2), zpart(3),
            pl.BlockSpec((c, hd // 2), lambda i: (i, 0)),
            pl.BlockSpec((c, hd // 2), lambda i: (i, 0)),
            whole(dm), whole(qd), whole(kd),
            pl.BlockSpec(memory_space=pltpu.SMEM),
            pl.BlockSpec((None, 1, rw), lambda i: (layer, 0, 0)),
        ],
        out_specs=pl.BlockSpec((c, rw), lambda i: (i, 0)),
        scratch_shapes=[pltpu.VMEM((nh, hd, hd), F32)],
        compiler_params=_cparams("arbitrary"),
        name="retention",
    )(z, z, z, z, cos, sin, dm, qd, kd, cd, gain)


def _retention_tables(c, hd):
    lg = jnp.log(1.0 - 2.0 ** (-5.0 - jnp.arange(RET_HEADS, dtype=F32)))
    n = jnp.arange(c, dtype=F32)
    rel = n[:, None] - n[None, :]
    dm = jnp.where(rel >= 0, jnp.exp(lg[:, None, None] * jnp.maximum(rel, 0.0)), 0.0)
    qd = jnp.broadcast_to(jnp.exp(lg[:, None] * (n + 1.0))[..., None], (RET_HEADS, c, hd // 2))
    kd = jnp.broadcast_to(jnp.exp(lg[:, None] * (c - 1.0 - n))[..., None], (RET_HEADS, c, hd // 2))
    cd = jnp.exp(lg * c)
    return dm, qd, kd, cd


def _rope_tables(seq, hd):
    angle = 1.0 / (ROPE_BASE ** jnp.linspace(0.0, 1.0, hd // 2, dtype=F32))
    ang = jnp.arange(seq, dtype=F32)[:, None] * angle[None, :]
    return jnp.cos(ang), jnp.sin(ang)


def _s5_prep_kernel(are_ref, aim_ref, ls_ref, are_t, aim_t, ls_t, wbr_ref, wbi_ref, wb_ref, lam_ref):
    ns = are_ref.shape[-1]

    def lam_bar(ar, ai, ls):
        delta = jnp.exp(ls)
        mag = jnp.exp(ar * delta)
        return mag * jnp.cos(ai * delta), mag * jnp.sin(ai * delta)

    ar = are_ref[...]
    ai = aim_ref[...]
    lr, li = lam_bar(ar, ai, ls_ref[...])
    nr, ni = lr - 1.0, li
    inv = 1.0 / (ar * ar + ai * ai)
    kr = (nr * ar + ni * ai) * inv
    ki = (ni * ar - nr * ai) * inv
    br = wbr_ref[...]
    bi = wbi_ref[...]
    wb_ref[:, :ns] = (kr * br - ki * bi).astype(BF16)
    wb_ref[:, ns:] = (kr * bi + ki * br).astype(BF16)
    tr, ti = lam_bar(are_t[...], aim_t[...], ls_t[...])
    lam_ref[0] = tr
    lam_ref[1] = ti


def _s5_prep(prm):
    are, aim, ls, are_t, aim_t, ls_t, wbr, wbi = prm
    depth, nb, w, ns = wbr.shape
    nst = are_t.shape[2]

    def blk(*shape):
        return pl.BlockSpec((None, None) + shape, lambda l, j: (l, j) + (0,) * len(shape))

    return pl.pallas_call(
        _s5_prep_kernel,
        out_shape=(jax.ShapeDtypeStruct((depth, nb, w, 2 * ns), BF16),
                   jax.ShapeDtypeStruct((depth, nb, 2, nst, V7X_LANES), F32)),
        grid=(depth, nb),
        in_specs=[blk(1, ns), blk(1, ns), blk(1, ns),
                  blk(nst, V7X_LANES), blk(nst, V7X_LANES), blk(nst, V7X_LANES),
                  blk(w, ns), blk(w, ns)],
        out_specs=(blk(w, 2 * ns), blk(2, nst, V7X_LANES)),
        compiler_params=_cparams("parallel", "parallel"),
        name="s5_prep",
    )(are, aim, ls, are_t, aim_t, ls_t, wbr, wbi)


def _s5_kernel(u_ref, wb_ref, lam_ref, wcr_ref, wci_ref, d_ref, wg_ref, bg_ref, gs_ref, y_ref,
               x_s, hx_s, h_s, y_s):
    nb, w = wb_ref.shape[0], wb_ref.shape[1]
    ns = wcr_ref.shape[1]
    nst = ns // V7X_LANES
    rows = u_ref.shape[0]
    pitch = x_s.shape[1] // nst
    group = x_s.shape[0] // 2

    @pl.when(pl.program_id(0) == 0)
    def _():
        h_s[...] = jnp.zeros_like(h_s)

    for first in range(0, nb, group):
        blocks = range(first, first + group)
        for k, jb in enumerate(blocks):
            cols = slice(jb * w, (jb + 1) * w)
            bu = jnp.dot(u_ref[:, cols].astype(BF16), wb_ref[jb], preferred_element_type=F32)
            for part in range(2):
                for s in range(nst):
                    lo = part * ns + s * V7X_LANES
                    x_s[2 * k + part, s * pitch:s * pitch + rows, :] = bu[:, lo:lo + V7X_LANES]

        lam = [(lam_ref[jb, 0], lam_ref[jb, 1]) for jb in blocks]

        def step(t, carry):
            out = []
            now = pl.ds(t, nst, stride=pitch)
            here = pl.ds(pl.multiple_of(t * nst, nst), nst)
            for k in range(group):
                hr, hi = carry[k]
                lr, li = lam[k]
                nr = (lr * hr - li * hi) + x_s[2 * k, now, :]
                ni = (lr * hi + li * hr) + x_s[2 * k + 1, now, :]
                hx_s[2 * k, here, :] = nr
                hx_s[2 * k + 1, here, :] = ni
                out.append((nr, ni))
            return tuple(out)

        init = tuple((h_s[2 * jb], h_s[2 * jb + 1]) for jb in blocks)
        fin = lax.fori_loop(0, rows, step, init, unroll=8)
        for k, jb in enumerate(blocks):
            h_s[2 * jb] = fin[k][0]
            h_s[2 * jb + 1] = fin[k][1]

        for k, jb in enumerate(blocks):
            def states(part):
                return jnp.concatenate(
                    [hx_s[2 * k + part, pl.ds(s, rows, stride=nst), :] for s in range(nst)], axis=1).astype(BF16)

            y = (jnp.dot(states(0), wcr_ref[jb], preferred_element_type=F32)
                 - jnp.dot(states(1), wci_ref[jb], preferred_element_type=F32))
            cols = slice(jb * w, (jb + 1) * w)
            y_s[:, cols] = y + d_ref[:, cols] * u_ref[:, cols]

    y = y_s[...]
    zg = jnp.dot(jax.nn.gelu(y).astype(BF16), wg_ref[...], preferred_element_type=F32) + bg_ref[...]
    y = y * jax.nn.sigmoid(zg)
    y_ref[...] = (y * _rms_scale(y) * gs_ref[...]).astype(y_ref.dtype)


def _s5(z, wb, lam, wcr, wci, dsk, w_glu, b_glu, g_s5, layer, col0, sw, t):
    seq = z.shape[0]
    nb, w, ns2 = wb.shape[1], wb.shape[2], wb.shape[3]
    ns = ns2 // 2
    nst = ns // V7X_LANES
    rows = t["s5_rows"]
    assert col0 % sw == 0
    pitch = rows + V7X_SUBLANES
    group = math.gcd(nb, t["s5_group"])

    return pl.pallas_call(
        _s5_kernel,
        out_shape=jax.ShapeDtypeStruct((seq, sw), BF16),
        grid=(seq // rows,),
        in_specs=[
            pl.BlockSpec((rows, sw), lambda i: (i, col0 // sw)),
            _resident((None, nb, w, ns2), lambda i: (layer, 0, 0, 0)),
            _resident((None, nb, 2, nst, V7X_LANES), lambda i: (layer, 0, 0, 0, 0)),
            _resident((None, nb, ns, w), lambda i: (layer, 0, 0, 0)),
            _resident((None, nb, ns, w), lambda i: (layer, 0, 0, 0)),
            _resident((None, 1, sw), lambda i: (layer, 0, 0)),
            _resident((None, sw, sw), lambda i: (layer, 0, 0)),
            _resident((None, 1, sw), lambda i: (layer, 0, 0)),
            _resident((None, 1, sw), lambda i: (layer, 0, 0)),
        ],
        out_specs=pl.BlockSpec((rows, sw), lambda i: (i, 0)),
        scratch_shapes=[
            pltpu.VMEM((2 * group, nst * pitch, V7X_LANES), F32),
            pltpu.VMEM((2 * group, nst * rows, V7X_LANES), F32),
            pltpu.VMEM((2 * nb, nst, V7X_LANES), F32),
            pltpu.VMEM((rows, sw), F32),
        ],
        compiler_params=_cparams("arbitrary"),
        name="s5",
    )(z, wb, lam, wcr, wci, dsk, w_glu, b_glu, g_s5)


def _s5_layout(a_re, a_im, b_re, b_im, c_re, c_im, log_step):
    depth, groups, p = a_re.shape
    ch = b_re.shape[-1]
    gb = V7X_MXU_DIM // ch
    nb = groups // gb
    ns = gb * p
    eye = jnp.eye(gb, dtype=F32)

    def lane(v):
        return v.astype(F32).reshape(depth, nb, 1, ns)

    def tile(v):
        return v.astype(F32).reshape(depth, nb, ns // V7X_LANES, V7X_LANES)

    def bdiag_b(b):
        b5 = b.astype(F32).reshape(depth, nb, gb, p, ch)
        return jnp.einsum("ljgpc,gh->ljgchp", b5, eye).reshape(depth, nb, gb * ch, ns)

    def bdiag_c(c):
        c5 = c.astype(F32).reshape(depth, nb, gb, ch, p)
        return jnp.einsum("ljgcp,gh->ljgphc", c5, eye).reshape(depth, nb, ns, gb * ch).astype(BF16)

    ls = jnp.broadcast_to(log_step.astype(F32)[..., None], (depth, groups, p))
    prep = (lane(a_re), lane(a_im), lane(ls), tile(a_re), tile(a_im), tile(ls), bdiag_b(b_re), bdiag_b(b_im))
    return prep, bdiag_c(c_re), bdiag_c(c_im)


def _outproj_kernel(yr_ref, ys_ref, h_ref, wo_ref, o_ref):
    rw = yr_ref.shape[1]
    mix = (jnp.dot(yr_ref[...], wo_ref[:rw, :], preferred_element_type=F32)
           + jnp.dot(ys_ref[...], wo_ref[rw:, :], preferred_element_type=F32))
    o_ref[...] = h_ref[...] + mix


def _outproj(h, y_ret, y_ssm, w_out, layer, t):
    seq, d = h.shape
    rw, sw = y_ret.shape[1], y_ssm.shape[1]
    nblk, tn = w_out.shape[1], w_out.shape[3]
    tm = t["tm_proj"]
    return pl.pallas_call(
        _outproj_kernel,
        out_shape=jax.ShapeDtypeStruct((seq, d), F32),
        grid=(seq // tm, nblk),
        in_specs=[
            pl.BlockSpec((tm, rw), lambda i, j: (i, 0)),
            pl.BlockSpec((tm, sw), lambda i, j: (i, 0)),
            pl.BlockSpec((tm, tn), lambda i, j: (i, j)),
            pl.BlockSpec((None, None, rw + sw, tn), lambda i, j: (layer, j, 0, 0)),
        ],
        out_specs=pl.BlockSpec((tm, tn), lambda i, j: (i, j)),
        compiler_params=_cparams("parallel", "arbitrary"),
        name="outproj",
    )(y_ret, y_ssm, h, w_out)


def _cast_w13_kernel(w1_ref, w3_ref, o_ref, *, f_valid):
    tf = w1_ref.shape[1]
    col = pl.program_id(1) * tf + lax.broadcasted_iota(jnp.int32, (1, tf), 1)
    ok = col < f_valid
    o_ref[:, :tf] = jnp.where(ok, w1_ref[...], 0.0).astype(BF16)
    o_ref[:, tf:] = jnp.where(ok, w3_ref[...], 0.0).astype(BF16)


def _cast_w2_kernel(w2_ref, o_ref, *, f_valid):
    tf = w2_ref.shape[0]
    row = pl.program_id(1) * tf + lax.broadcasted_iota(jnp.int32, (tf, 1), 0)
    o_ref[...] = jnp.where(row < f_valid, w2_ref[...], 0.0).astype(BF16)


def _cast_block_kernel(w_ref, o_ref):
    o_ref[...] = w_ref[...].astype(BF16)


def _prep_ffn(w1, w3, w2, tf):
    depth, d, f = w1.shape
    nf = pl.cdiv(f, tf)
    w13 = pl.pallas_call(
        functools.partial(_cast_w13_kernel, f_valid=f),
        out_shape=jax.ShapeDtypeStruct((depth, nf, d, 2 * tf), BF16),
        grid=(depth, nf),
        in_specs=[pl.BlockSpec((None, d, tf), lambda l, j: (l, 0, j)),
                  pl.BlockSpec((None, d, tf), lambda l, j: (l, 0, j))],
        out_specs=pl.BlockSpec((None, None, d, 2 * tf), lambda l, j: (l, j, 0, 0)),
        compiler_params=_cparams("parallel", "parallel"),
        name="cast_w13",
    )(w1, w3)
    w2p = pl.pallas_call(
        functools.partial(_cast_w2_kernel, f_valid=f),
        out_shape=jax.ShapeDtypeStruct((depth, nf * tf, d), BF16),
        grid=(depth, nf),
        in_specs=[pl.BlockSpec((None, tf, d), lambda l, j: (l, j, 0))],
        out_specs=pl.BlockSpec((None, tf, d), lambda l, j: (l, j, 0)),
        compiler_params=_cparams("parallel", "parallel"),
        name="cast_w2",
    )(w2)
    return w13, w2p


def _cast_w_in_kernel(w_ref, p_ref, o_ref, *, n_perm):
    j = pl.program_id(1)

    @pl.when(j < n_perm)
    def _():
        o_ref[...] = jnp.dot(w_ref[...].astype(BF16), p_ref[...], preferred_element_type=F32).astype(BF16)

    @pl.when(j >= n_perm)
    def _():
        o_ref[...] = w_ref[...].astype(BF16)


def _prep_w_in(w_in, rw, tn):
    depth, d, n = w_in.shape
    hd = rw // RET_HEADS
    per = tn // hd
    src = jnp.arange(hd)
    dst = (src % 2) * (hd // 2) + src // 2
    perm = (dst[:, None] == jnp.arange(hd)[None, :]).astype(BF16)
    return pl.pallas_call(
        functools.partial(_cast_w_in_kernel, n_perm=2 * RET_HEADS),
        out_shape=jax.ShapeDtypeStruct((depth, n // tn, d, tn), BF16),
        grid=(depth, n // hd),
        in_specs=[pl.BlockSpec((None, d, hd), lambda l, j: (l, 0, j)),
                  pl.BlockSpec((hd, hd), lambda l, j: (0, 0))],
        out_specs=pl.BlockSpec((None, None, d, hd), lambda l, j: (l, j // per, 0, j % per)),
        compiler_params=_cparams("parallel", "parallel"),
        name="cast_w_in",
    )(w_in, perm)


def _prep_blocked(w, tn):
    depth, k, n = w.shape
    return pl.pallas_call(
        _cast_block_kernel,
        out_shape=jax.ShapeDtypeStruct((depth, n // tn, k, tn), BF16),
        grid=(depth, n // tn),
        in_specs=[pl.BlockSpec((None, k, tn), lambda l, j: (l, 0, j))],
        out_specs=pl.BlockSpec((None, None, k, tn), lambda l, j: (l, j, 0, 0)),
        compiler_params=_cparams("parallel", "parallel"),
        name="cast_blocked",
    )(w)


def kernel(x, ffn1_norm, ffn1_w1, ffn1_w3, ffn1_w2, mix_norm, w_in, s5_a_re, s5_a_im, s5_b_re, s5_b_im,
           s5_c_re, s5_c_im, s5_d, s5_log_step, s5_w_glu, s5_b_glu, s5_out_norm, ret_out_norm, w_out,
           ffn2_norm, ffn2_w1, ffn2_w3, ffn2_w2, final_norm):
    batch, seq, d = x.shape
    assert batch == 1, "sequence kernels carry state along the row axis of a single sequence"
    depth = w_in.shape[0]
    sw = s5_d.shape[-1]
    rw = ret_out_norm.shape[-1]
    assert w_in.shape[-1] == 4 * rw + sw and rw % V7X_MXU_DIM == 0
    hd = rw // RET_HEADS
    t = _tiles(seq, d, w_in.shape[-1])

    ffn_a = _prep_ffn(ffn1_w1, ffn1_w3, ffn1_w2, t["tf"])
    ffn_b = _prep_ffn(ffn2_w1, ffn2_w3, ffn2_w2, t["tf"])
    w_in_b = _prep_w_in(w_in, rw, t["tn_in"])
    w_glu_b = s5_w_glu.astype(BF16)
    w_out_b = _prep_blocked(w_out, t["tn_out"])
    s5_raw, wcr, wci = _s5_layout(s5_a_re, s5_a_im, s5_b_re, s5_b_im, s5_c_re, s5_c_im, s5_log_step)
    wb, lam = _s5_prep(s5_raw)
    cos, sin = _rope_tables(seq, hd)
    ret_tabs = _retention_tables(t["ret_chunk"], hd)

    def row3(v):
        return v.astype(F32)[:, None, :]

    g1, gm, g2 = row3(ffn1_norm), row3(mix_norm), row3(ffn2_norm)
    bg, gs, gr, dsk = row3(s5_b_glu), row3(s5_out_norm), row3(ret_out_norm), row3(s5_d)
    gf = final_norm.astype(F32)[None, :]

    h = x.reshape(seq, d)
    for l in range(depth):
        h = _ffn(h, g1, ffn_a, l, t, gf)
        z = _inproj(h, gm, w_in_b, l, t)
        y_ret = _retention(z, cos, sin, ret_tabs, gr, l, rw, t)
        y_ssm = _s5(z, wb, lam, wcr, wci, dsk, w_glu_b, bg, gs, l, 4 * rw, sw, t)
        h = _outproj(h, y_ret, y_ssm, w_out_b, l, t)
        h = _ffn(h, g2, ffn_b, l, t, gf, final_norm=(l == depth - 1))
    return h.reshape(batch, seq, d)
```

```python
import functools
import math

import jax
import jax.numpy as jnp
from jax import lax
from jax.experimental import pallas as pl
from jax.experimental.pallas import tpu as pltpu

F32 = jnp.float32
BF16 = jnp.bfloat16

NORM_EPS = 1e-6
RET_HEADS = 8
ROPE_BASE = 10000.0

V7X_SUBLANES = 8
V7X_LANES = 128
V7X_MXU_DIM = 256
V7X_VMEM_LIMIT_BYTES = 62 * 1024 * 1024


def _divisor_tile(n, want):
    for cand in range(want, 0, -V7X_MXU_DIM):
        if n % cand == 0:
            return cand
    raise ValueError(f"no tile <= {want} divides {n}")


def _tiles(seq, d, n_in):
    return dict(
        tm=_divisor_tile(seq, 512),
        tf=512,
        tm_proj=_divisor_tile(seq, 1024),
        tn_in=_divisor_tile(n_in, 1024),
        tn_out=_divisor_tile(d, 1024),
        ret_chunk=_divisor_tile(seq, 256),
        s5_rows=_divisor_tile(seq, 256),
        s5_group=4,
    )


NORM_SLAB_ROWS = 256


def _cparams(*sem):
    return pltpu.CompilerParams(dimension_semantics=sem, vmem_limit_bytes=V7X_VMEM_LIMIT_BYTES)


def _rms_scale(x):
    return lax.rsqrt(jnp.mean(x * x, axis=-1, keepdims=True) + NORM_EPS)


def _slabs(rows):
    step = min(NORM_SLAB_ROWS, rows)
    return [slice(r, r + step) for r in range(0, rows, step)]


def _resident(shape, index_map):
    return pl.BlockSpec(shape, index_map, pipeline_mode=pl.Buffered(1))


def _row_tile_prefetch(h_hbm, h_s, sem, consume):
    i, j = pl.program_id(0), pl.program_id(1)
    tm = h_s.shape[0]

    def tile_copy(tile):
        return pltpu.make_async_copy(h_hbm.at[pl.ds(pl.multiple_of(tile * tm, tm), tm), :], h_s, sem)

    @pl.when((i == 0) & (j == 0))
    def _():
        tile_copy(0).start()

    @pl.when(j == 0)
    def _():
        tile_copy(i).wait()
        consume()

    @pl.when((j == 1) & (i + 1 < pl.num_programs(0)))
    def _():
        tile_copy(i + 1).start()


def _ffn_kernel(h_hbm, g_ref, w13_ref, w2_ref, gf_ref, o_ref, a_s, h_s, sem, *, n_split, final_norm):
    d = o_ref.shape[1]
    dn = d // n_split

    def gated(a):
        r = jnp.dot(a, w13_ref[...], preferred_element_type=F32)
        tf = r.shape[1] // 2
        gate = r[:, :tf]
        return (gate * jax.nn.sigmoid(gate) * r[:, tf:] * 0.5).astype(BF16)

    def start_tile():
        for rows in _slabs(h_s.shape[0]):
            h = h_s[rows, :]
            a = (h * _rms_scale(h) * g_ref[...]).astype(BF16)
            a_s[rows, :] = a
            p = gated(a)
            for s in range(n_split):
                cols = slice(s * dn, (s + 1) * dn)
                o_ref[rows, cols] = h_s[rows, cols] + jnp.dot(p, w2_ref[:, cols], preferred_element_type=F32)

    _row_tile_prefetch(h_hbm, h_s, sem, start_tile)

    @pl.when(pl.program_id(1) > 0)
    def _():
        p = gated(a_s[...])
        for s in range(n_split):
            cols = slice(s * dn, (s + 1) * dn)
            o_ref[:, cols] += jnp.dot(p, w2_ref[:, cols], preferred_element_type=F32)

    if final_norm:
        @pl.when(pl.program_id(1) == pl.num_programs(1) - 1)
        def _():
            for rows in _slabs(o_ref.shape[0]):
                hf = o_ref[rows, :]
                o_ref[rows, :] = hf * _rms_scale(hf) * gf_ref[...]


def _ffn(h, gain, weights, layer, t, final_gain, final_norm=False):
    seq, d = h.shape
    tm = t["tm"]
    w13, w2 = weights
    nf, tf = w13.shape[1], w13.shape[3] // 2
    assert nf >= 2
    return pl.pallas_call(
        functools.partial(_ffn_kernel, n_split=max(1, d // 1024), final_norm=final_norm),
        out_shape=jax.ShapeDtypeStruct((seq, d), F32),
        grid=(seq // tm, nf),
        in_specs=[
            pl.BlockSpec(memory_space=pl.ANY),
            pl.BlockSpec((None, 1, d), lambda i, f: (layer, 0, 0)),
            pl.BlockSpec((None, None, d, 2 * tf), lambda i, f: (layer, f, 0, 0)),
            pl.BlockSpec((None, tf, d), lambda i, f: (layer, f, 0)),
            pl.BlockSpec((1, d), lambda i, f: (0, 0)),
        ],
        out_specs=pl.BlockSpec((tm, d), lambda i, f: (i, 0)),
        scratch_shapes=[pltpu.VMEM((tm, d), BF16), pltpu.VMEM((tm, d), F32), pltpu.SemaphoreType.DMA(())],
        compiler_params=_cparams("arbitrary", "arbitrary"),
        name="ffn",
    )(h, gain, w13, w2, final_gain)


def _inproj_kernel(h_hbm, g_ref, w_ref, z_ref, a_s, h_s, sem):
    def start_tile():
        for rows in _slabs(h_s.shape[0]):
            h = h_s[rows, :]
            a = (h * _rms_scale(h) * g_ref[...]).astype(BF16)
            a_s[rows, :] = a
            z_ref[rows, :] = jnp.dot(a, w_ref[...], preferred_element_type=F32)

    _row_tile_prefetch(h_hbm, h_s, sem, start_tile)

    @pl.when(pl.program_id(1) > 0)
    def _():
        z_ref[...] = jnp.dot(a_s[...], w_ref[...], preferred_element_type=F32)


def _inproj(h, gain, w_in, layer, t):
    seq, d = h.shape
    nblk, tn = w_in.shape[1], w_in.shape[3]
    n = nblk * tn
    tm = t["tm_proj"]
    assert nblk >= 2
    return pl.pallas_call(
        _inproj_kernel,
        out_shape=jax.ShapeDtypeStruct((seq, n), F32),
        grid=(seq // tm, nblk),
        in_specs=[
            pl.BlockSpec(memory_space=pl.ANY),
            pl.BlockSpec((None, 1, d), lambda i, j: (layer, 0, 0)),
            pl.BlockSpec((None, None, d, tn), lambda i, j: (layer, j, 0, 0)),
        ],
        out_specs=pl.BlockSpec((tm, tn), lambda i, j: (i, j)),
        scratch_shapes=[pltpu.VMEM((tm, d), BF16), pltpu.VMEM((tm, d), F32), pltpu.SemaphoreType.DMA(())],
        compiler_params=_cparams("arbitrary", "arbitrary"),
        name="inproj",
    )(h, gain, w_in)


def _ret_kernel(q_ref, k_ref, v_ref, g_ref, cos_ref, sin_ref, dm_ref, qd_ref, kd_ref, cd_ref, gain_ref,
                o_ref, s_s, *, k_scale):
    @pl.when(pl.program_id(0) == 0)
    def _():
        s_s[...] = jnp.zeros_like(s_s)

    nh, hd = s_s.shape[0], s_s.shape[1]
    half = hd // 2
    cos = cos_ref[...]
    sin = sin_ref[...]

    def rope(t):
        te, to = t[:, :half], t[:, half:]
        return jnp.concatenate([te * cos - to * sin, to * cos + te * sin], axis=1)

    for h in range(nh):
        cols = slice(h * hd, (h + 1) * hd)
        qr = rope(q_ref[:, cols])
        kr = rope(k_ref[:, cols]) * k_scale
        qb = qr.astype(BF16)
        kb = kr.astype(BF16)
        vb = v_ref[:, cols].astype(BF16)
        scores = lax.dot_general(qb, kb, (((1,), (1,)), ((), ())), preferred_element_type=F32)
        scores = (scores * dm_ref[h]).astype(BF16)
        inner = jnp.dot(scores, vb, preferred_element_type=F32)
        state = s_s[h]
        qd = qd_ref[h]
        kd = kd_ref[h]
        cross = (jnp.dot(qb, state.astype(BF16), preferred_element_type=F32)
                 * jnp.concatenate([qd, qd], axis=1))
        kdec = (kr * jnp.concatenate([kd, kd], axis=1)).astype(BF16)
        s_s[h] = state * cd_ref[h] + lax.dot_general(kdec, vb, (((0,), (0,)), ((), ())),
                                                     preferred_element_type=F32)
        out = inner + cross
        out = out * _rms_scale(out) * gain_ref[:, cols]
        g = g_ref[:, cols]
        o_ref[:, cols] = (out * (g * jax.nn.sigmoid(g))).astype(o_ref.dtype)


def _retention(z, cos, sin, tabs, gain, layer, rw, t):
    seq = z.shape[0]
    nh = RET_HEADS
    hd = rw // nh
    c = t["ret_chunk"]
    dm, qd, kd, cd = tabs

    def zpart(part):
        return pl.BlockSpec((c, rw), lambda i: (i, part))

    def whole(a):
        return pl.BlockSpec(a.shape, lambda i: (0,) * a.ndim)

    return pl.pallas_call(
        functools.partial(_ret_kernel, k_scale=float(hd) ** -0.5),
        out_shape=jax.ShapeDtypeStruct((seq, rw), BF16),
        grid=(seq // c,),
        in_specs=[
            zpart(0), zpart(1), zpart(2), zpart(3),
            pl.BlockSpec((c, hd // 2), lambda i: (i, 0)),
            pl.BlockSpec((c, hd // 2), lambda i: (i, 0)),
            whole(dm), whole(qd), whole(kd),
            pl.BlockSpec(memory_space=pltpu.SMEM),
            pl.BlockSpec((None, 1, rw), lambda i: (layer, 0, 0)),
        ],
        out_specs=pl.BlockSpec((c, rw), lambda i: (i, 0)),
        scratch_shapes=[pltpu.VMEM((nh, hd, hd), F32)],
        compiler_params=_cparams("arbitrary"),
        name="retention",
    )(z, z, z, z, cos, sin, dm, qd, kd, cd, gain)


def _retention_tables(c, hd):
    lg = jnp.log(1.0 - 2.0 ** (-5.0 - jnp.arange(RET_HEADS, dtype=F32)))
    n = jnp.arange(c, dtype=F32)
    rel = n[:, None] - n[None, :]
    dm = jnp.where(rel >= 0, jnp.exp(lg[:, None, None] * jnp.maximum(rel, 0.0)), 0.0)
    qd = jnp.broadcast_to(jnp.exp(lg[:, None] * (n + 1.0))[..., None], (RET_HEADS, c, hd // 2))
    kd = jnp.broadcast_to(jnp.exp(lg[:, None] * (c - 1.0 - n))[..., None], (RET_HEADS, c, hd // 2))
    cd = jnp.exp(lg * c)
    return dm, qd, kd, cd


def _rope_tables(seq, hd):
    angle = 1.0 / (ROPE_BASE ** jnp.linspace(0.0, 1.0, hd // 2, dtype=F32))
    ang = jnp.arange(seq, dtype=F32)[:, None] * angle[None, :]
    return jnp.cos(ang), jnp.sin(ang)


def _s5_prep_kernel(are_ref, aim_ref, ls_ref, are_t, aim_t, ls_t, wbr_ref, wbi_ref, wb_ref, lam_ref):
    ns = are_ref.shape[-1]

    def lam_bar(ar, ai, ls):
        delta = jnp.exp(ls)
        mag = jnp.exp(ar * delta)
        return mag * jnp.cos(ai * delta), mag * jnp.sin(ai * delta)

    ar = are_ref[...]
    ai = aim_ref[...]
    lr, li = lam_bar(ar, ai, ls_ref[...])
    nr, ni = lr - 1.0, li
    inv = 1.0 / (ar * ar + ai * ai)
    kr = (nr * ar + ni * ai) * inv
    ki = (ni * ar - nr * ai) * inv
    br = wbr_ref[...]
    bi = wbi_ref[...]
    wb_ref[:, :ns] = (kr * br - ki * bi).astype(BF16)
    wb_ref[:, ns:] = (kr * bi + ki * br).astype(BF16)
    tr, ti = lam_bar(are_t[...], aim_t[...], ls_t[...])
    lam_ref[0] = tr
    lam_ref[1] = ti


def _s5_prep(prm):
    are, aim, ls, are_t, aim_t, ls_t, wbr, wbi = prm
    depth, nb, w, ns = wbr.shape
    nst = are_t.shape[2]

    def blk(*shape):
        return pl.BlockSpec((None, None) + shape, lambda l, j: (l, j) + (0,) * len(shape))

    return pl.pallas_call(
        _s5_prep_kernel,
        out_shape=(jax.ShapeDtypeStruct((depth, nb, w, 2 * ns), BF16),
                   jax.ShapeDtypeStruct((depth, nb, 2, nst, V7X_LANES), F32)),
        grid=(depth, nb),
        in_specs=[blk(1, ns), blk(1, ns), blk(1, ns),
                  blk(nst, V7X_LANES), blk(nst, V7X_LANES), blk(nst, V7X_LANES),
                  blk(w, ns), blk(w, ns)],
        out_specs=(blk(w, 2 * ns), blk(2, nst, V7X_LANES)),
        compiler_params=_cparams("parallel", "parallel"),
        name="s5_prep",
    )(are, aim, ls, are_t, aim_t, ls_t, wbr, wbi)


def _s5_kernel(u_ref, wb_ref, lam_ref, wcr_ref, wci_ref, d_ref, wg_ref, bg_ref, gs_ref, y_ref,
               x_s, hx_s, h_s, y_s):
    nb, w = wb_ref.shape[0], wb_ref.shape[1]
    ns = wcr_ref.shape[1]
    nst = ns // V7X_LANES
    rows = u_ref.shape[0]
    pitch = x_s.shape[1] // nst
    group = x_s.shape[0] // 2

    @pl.when(pl.program_id(0) == 0)
    def _():
        h_s[...] = jnp.zeros_like(h_s)

    for first in range(0, nb, group):
        blocks = range(first, first + group)
        for k, jb in enumerate(blocks):
            cols = slice(jb * w, (jb + 1) * w)
            bu = jnp.dot(u_ref[:, cols].astype(BF16), wb_ref[jb], preferred_element_type=F32)
            for part in range(2):
                for s in range(nst):
                    lo = part * ns + s * V7X_LANES
                    x_s[2 * k + part, s * pitch:s * pitch + rows, :] = bu[:, lo:lo + V7X_LANES]

        lam = [(lam_ref[jb, 0], lam_ref[jb, 1]) for jb in blocks]

        def step(t, carry):
            out = []
            now = pl.ds(t, nst, stride=pitch)
            here = pl.ds(pl.multiple_of(t * nst, nst), nst)
            for k in range(group):
                hr, hi = carry[k]
                lr, li = lam[k]
                nr = (lr * hr - li * hi) + x_s[2 * k, now, :]
                ni = (lr * hi + li * hr) + x_s[2 * k + 1, now, :]
                hx_s[2 * k, here, :] = nr
                hx_s[2 * k + 1, here, :] = ni
                out.append((nr, ni))
            return tuple(out)

        init = tuple((h_s[2 * jb], h_s[2 * jb + 1]) for jb in blocks)
        fin = lax.fori_loop(0, rows, step, init, unroll=8)
        for k, jb in enumerate(blocks):
            h_s[2 * jb] = fin[k][0]
            h_s[2 * jb + 1] = fin[k][1]

        for k, jb in enumerate(blocks):
            def states(part):
                return jnp.concatenate(
                    [hx_s[2 * k + part, pl.ds(s, rows, stride=nst), :] for s in range(nst)], axis=1).astype(BF16)

            y = (jnp.dot(states(0), wcr_ref[jb], preferred_element_type=F32)
                 - jnp.dot(states(1), wci_ref[jb], preferred_element_type=F32))
            cols = slice(jb * w, (jb + 1) * w)
            y_s[:, cols] = y + d_ref[:, cols] * u_ref[:, cols]

    y = y_s[...]
    zg = jnp.dot(jax.nn.gelu(y).astype(BF16), wg_ref[...], preferred_element_type=F32) + bg_ref[...]
    y = y * jax.nn.sigmoid(zg)
    y_ref[...] = (y * _rms_scale(y) * gs_ref[...]).astype(y_ref.dtype)


def _s5(z, wb, lam, wcr, wci, dsk, w_glu, b_glu, g_s5, layer, col0, sw, t):
    seq = z.shape[0]
    nb, w, ns2 = wb.shape[1], wb.shape[2], wb.shape[3]
    ns = ns2 // 2
    nst = ns // V7X_LANES
    rows = t["s5_rows"]
    assert col0 % sw == 0
    pitch = rows + V7X_SUBLANES
    group = math.gcd(nb, t["s5_group"])

    return pl.pallas_call(
        _s5_kernel,
        out_shape=jax.ShapeDtypeStruct((seq, sw), BF16),
        grid=(seq // rows,),
        in_specs=[
            pl.BlockSpec((rows, sw), lambda i: (i, col0 // sw)),
            _resident((None, nb, w, ns2), lambda i: (layer, 0, 0, 0)),
            _resident((None, nb, 2, nst, V7X_LANES), lambda i: (layer, 0, 0, 0, 0)),
            _resident((None, nb, ns, w), lambda i: (layer, 0, 0, 0)),
            _resident((None, nb, ns, w), lambda i: (layer, 0, 0, 0)),
            _resident((None, 1, sw), lambda i: (layer, 0, 0)),
            _resident((None, sw, sw), lambda i: (layer, 0, 0)),
            _resident((None, 1, sw), lambda i: (layer, 0, 0)),
            _resident((None, 1, sw), lambda i: (layer, 0, 0)),
        ],
        out_specs=pl.BlockSpec((rows, sw), lambda i: (i, 0)),
        scratch_shapes=[
            pltpu.VMEM((2 * group, nst * pitch, V7X_LANES), F32),
            pltpu.VMEM((2 * group, nst * rows, V7X_LANES), F32),
            pltpu.VMEM((2 * nb, nst, V7X_LANES), F32),
            pltpu.VMEM((rows, sw), F32),
        ],
        compiler_params=_cparams("arbitrary"),
        name="s5",
    )(z, wb, lam, wcr, wci, dsk, w_glu, b_glu, g_s5)


def _s5_layout(a_re, a_im, b_re, b_im, c_re, c_im, log_step):
    depth, groups, p = a_re.shape
    ch = b_re.shape[-1]
    gb = V7X_MXU_DIM // ch
    nb = groups // gb
    ns = gb * p
    eye = jnp.eye(gb, dtype=F32)

    def lane(v):
        return v.astype(F32).reshape(depth, nb, 1, ns)

    def tile(v):
        return v.astype(F32).reshape(depth, nb, ns // V7X_LANES, V7X_LANES)

    def bdiag_b(b):
        b5 = b.astype(F32).reshape(depth, nb, gb, p, ch)
        return jnp.einsum("ljgpc,gh->ljgchp", b5, eye).reshape(depth, nb, gb * ch, ns)

    def bdiag_c(c):
        c5 = c.astype(F32).reshape(depth, nb, gb, ch, p)
        return jnp.einsum("ljgcp,gh->ljgphc", c5, eye).reshape(depth, nb, ns, gb * ch).astype(BF16)

    ls = jnp.broadcast_to(log_step.astype(F32)[..., None], (depth, groups, p))
    prep = (lane(a_re), lane(a_im), lane(ls), tile(a_re), tile(a_im), tile(ls), bdiag_b(b_re), bdiag_b(b_im))
    return prep, bdiag_c(c_re), bdiag_c(c_im)


def _outproj_kernel(yr_ref, ys_ref, h_ref, wo_ref, o_ref):
    rw = yr_ref.shape[1]
    mix = (jnp.dot(yr_ref[...], wo_ref[:rw, :], preferred_element_type=F32)
           + jnp.dot(ys_ref[...], wo_ref[rw:, :], preferred_element_type=F32))
    o_ref[...] = h_ref[...] + mix


def _outproj(h, y_ret, y_ssm, w_out, layer, t):
    seq, d = h.shape
    rw, sw = y_ret.shape[1], y_ssm.shape[1]
    nblk, tn = w_out.shape[1], w_out.shape[3]
    tm = t["tm_proj"]
    return pl.pallas_call(
        _outproj_kernel,
        out_shape=jax.ShapeDtypeStruct((seq, d), F32),
        grid=(seq // tm, nblk),
        in_specs=[
            pl.BlockSpec((tm, rw), lambda i, j: (i, 0)),
            pl.BlockSpec((tm, sw), lambda i, j: (i, 0)),
            pl.BlockSpec((tm, tn), lambda i, j: (i, j)),
            pl.BlockSpec((None, None, rw + sw, tn), lambda i, j: (layer, j, 0, 0)),
        ],
        out_specs=pl.BlockSpec((tm, tn), lambda i, j: (i, j)),
        compiler_params=_cparams("parallel", "arbitrary"),
        name="outproj",
    )(y_ret, y_ssm, h, w_out)


def _cast_w13_kernel(w1_ref, w3_ref, o_ref, *, f_valid):
    tf = w1_ref.shape[1]
    col = pl.program_id(1) * tf + lax.broadcasted_iota(jnp.int32, (1, tf), 1)
    ok = col < f_valid
    o_ref[:, :tf] = jnp.where(ok, w1_ref[...], 0.0).astype(BF16)
    o_ref[:, tf:] = jnp.where(ok, w3_ref[...], 0.0).astype(BF16)


def _cast_w2_kernel(w2_ref, o_ref, *, f_valid):
    tf = w2_ref.shape[0]
    row = pl.program_id(1) * tf + lax.broadcasted_iota(jnp.int32, (tf, 1), 0)
    o_ref[...] = jnp.where(row < f_valid, w2_ref[...], 0.0).astype(BF16)


def _cast_block_kernel(w_ref, o_ref):
    o_ref[...] = w_ref[...].astype(BF16)


def _prep_ffn(w1, w3, w2, tf):
    depth, d, f = w1.shape
    nf = pl.cdiv(f, tf)
    w13 = pl.pallas_call(
        functools.partial(_cast_w13_kernel, f_valid=f),
        out_shape=jax.ShapeDtypeStruct((depth, nf, d, 2 * tf), BF16),
        grid=(depth, nf),
        in_specs=[pl.BlockSpec((None, d, tf), lambda l, j: (l, 0, j)),
                  pl.BlockSpec((None, d, tf), lambda l, j: (l, 0, j))],
        out_specs=pl.BlockSpec((None, None, d, 2 * tf), lambda l, j: (l, j, 0, 0)),
        compiler_params=_cparams("parallel", "parallel"),
        name="cast_w13",
    )(w1, w3)
    w2p = pl.pallas_call(
        functools.partial(_cast_w2_kernel, f_valid=f),
        out_shape=jax.ShapeDtypeStruct((depth, nf * tf, d), BF16),
        grid=(depth, nf),
        in_specs=[pl.BlockSpec((None, tf, d), lambda l, j: (l, j, 0))],
        out_specs=pl.BlockSpec((None, tf, d), lambda l, j: (l, j, 0)),
        compiler_params=_cparams("parallel", "parallel"),
        name="cast_w2",
    )(w2)
    return w13, w2p


def _cast_w_in_kernel(w_ref, p_ref, o_ref, *, n_perm):
    j = pl.program_id(1)

    @pl.when(j < n_perm)
    def _():
        o_ref[...] = jnp.dot(w_ref[...].astype(BF16), p_ref[...], preferred_element_type=F32).astype(BF16)

    @pl.when(j >= n_perm)
    def _():
        o_ref[...] = w_ref[...].astype(BF16)


def _prep_w_in(w_in, rw, tn):
    depth, d, n = w_in.shape
    hd = rw // RET_HEADS
    per = tn // hd
    src = jnp.arange(hd)
    dst = (src % 2) * (hd // 2) + src // 2
    perm = (dst[:, None] == jnp.arange(hd)[None, :]).astype(BF16)
    return pl.pallas_call(
        functools.partial(_cast_w_in_kernel, n_perm=2 * RET_HEADS),
        out_shape=jax.ShapeDtypeStruct((depth, n // tn, d, tn), BF16),
        grid=(depth, n // hd),
        in_specs=[pl.BlockSpec((None, d, hd), lambda l, j: (l, 0, j)),
                  pl.BlockSpec((hd, hd), lambda l, j: (0, 0))],
        out_specs=pl.BlockSpec((None, None, d, hd), lambda l, j: (l, j // per, 0, j % per)),
        compiler_params=_cparams("parallel", "parallel"),
        name="cast_w_in",
    )(w_in, perm)


def _prep_blocked(w, tn):
    depth, k, n = w.shape
    return pl.pallas_call(
        _cast_block_kernel,
        out_shape=jax.ShapeDtypeStruct((depth, n // tn, k, tn), BF16),
        grid=(depth, n // tn),
        in_specs=[pl.BlockSpec((None, k, tn), lambda l, j: (l, 0, j))],
        out_specs=pl.BlockSpec((None, None, k, tn), lambda l, j: (l, j, 0, 0)),
        compiler_params=_cparams("parallel", "parallel"),
        name="cast_blocked",
    )(w)


def kernel(x, ffn1_norm, ffn1_w1, ffn1_w3, ffn1_w2, mix_norm, w_in, s5_a_re, s5_a_im, s5_b_re, s5_b_im,
           s5_c_re, s5_c_im, s5_d, s5_log_step, s5_w_glu, s5_b_glu, s5_out_norm, ret_out_norm, w_out,
           ffn2_norm, ffn2_w1, ffn2_w3, ffn2_w2, final_norm):
    batch, seq, d = x.shape
    assert batch == 1, "sequence kernels carry state along the row axis of a single sequence"
    depth = w_in.shape[0]
    sw = s5_d.shape[-1]
    rw = ret_out_norm.shape[-1]
    assert w_in.shape[-1] == 4 * rw + sw and rw % V7X_MXU_DIM == 0
    hd = rw // RET_HEADS
    t = _tiles(seq, d, w_in.shape[-1])

    ffn_a = _prep_ffn(ffn1_w1, ffn1_w3, ffn1_w2, t["tf"])
    ffn_b = _prep_ffn(ffn2_w1, ffn2_w3, ffn2_w2, t["tf"])
    w_in_b = _prep_w_in(w_in, rw, t["tn_in"])
    w_glu_b = s5_w_glu.astype(BF16)
    w_out_b = _prep_blocked(w_out, t["tn_out"])
    s5_raw, wcr, wci = _s5_layout(s5_a_re, s5_a_im, s5_b_re, s5_b_im, s5_c_re, s5_c_im, s5_log_step)
    wb, lam = _s5_prep(s5_raw)
    cos, sin = _rope_tables(seq, hd)
    ret_tabs = _retention_tables(t["ret_chunk"], hd)

    def row3(v):
        return v.astype(F32)[:, None, :]

    g1, gm, g2 = row3(ffn1_norm), row3(mix_norm), row3(ffn2_norm)
    bg, gs, gr, dsk = row3(s5_b_glu), row3(s5_out_norm), row3(ret_out_norm), row3(s5_d)
    gf = final_norm.astype(F32)[None, :]

    h = x.reshape(seq, d)
    for l in range(depth):
        h = _ffn(h, g1, ffn_a, l, t, gf)
        z = _inproj(h, gm, w_in_b, l, t)
        y_ret = _retention(z, cos, sin, ret_tabs, gr, l, rw, t)
        y_ssm = _s5(z, wb, lam, wcr, wci, dsk, w_glu_b, bg, gs, l, 4 * rw, sw, t)
        h = _outproj(h, y_ret, y_ssm, w_out_b, l, t)
        h = _ffn(h, g2, ffn_b, l, t, gf, final_norm=(l == depth - 1))
    return h.reshape(batch, seq, d)
```

```python
import functools
import math

import jax
import jax.numpy as jnp
from jax import lax
from jax.experimental import pallas as pl
from jax.experimental.pallas import tpu as pltpu

F32 = jnp.float32
BF16 = jnp.bfloat16

NORM_EPS = 1e-6
RET_HEADS = 8
ROPE_BASE = 10000.0

V7X_SUBLANES = 8
V7X_LANES = 128
V7X_MXU_DIM = 256
V7X_VMEM_LIMIT_BYTES = 62 * 1024 * 1024


def _divisor_tile(n, want):
    for cand in range(want, 0, -V7X_MXU_DIM):
        if n % cand == 0:
            return cand
    raise ValueError(f"no tile <= {want} divides {n}")


def _tiles(seq, d, n_in):
    return dict(
        tm=_divisor_tile(seq, 512),
        tf=512,
        tm_proj=_divisor_tile(seq, 1024),
        tn_in=_divisor_tile(n_in, 1024),
        tn_out=_divisor_tile(d, 1024),
        ret_chunk=_divisor_tile(seq, 256),
        s5_rows=_divisor_tile(seq, 256),
        s5_group=4,
    )


NORM_SLAB_ROWS = 256


def _cparams(*sem):
    return pltpu.CompilerParams(dimension_semantics=sem, vmem_limit_bytes=V7X_VMEM_LIMIT_BYTES)


def _rms_scale(x):
    return lax.rsqrt(jnp.mean(x * x, axis=-1, keepdims=True) + NORM_EPS)


def _slabs(rows):
    step = min(NORM_SLAB_ROWS, rows)
    return [slice(r, r + step) for r in range(0, rows, step)]


def _resident(shape, index_map):
    return pl.BlockSpec(shape, index_map, pipeline_mode=pl.Buffered(1))


def _row_tile_prefetch(h_hbm, h_s, sem, consume):
    i, j = pl.program_id(0), pl.program_id(1)
    tm = h_s.shape[0]

    def tile_copy(tile):
        return pltpu.make_async_copy(h_hbm.at[pl.ds(pl.multiple_of(tile * tm, tm), tm), :], h_s, sem)

    @pl.when((i == 0) & (j == 0))
    def _():
        tile_copy(0).start()

    @pl.when(j == 0)
    def _():
        tile_copy(i).wait()
        consume()

    @pl.when((j == 1) & (i + 1 < pl.num_programs(0)))
    def _():
        tile_copy(i + 1).start()


def _ffn_kernel(h_hbm, g_ref, w13_ref, w2_ref, gf_ref, o_ref, a_s, h_s, sem, *, n_split, final_norm):
    d = o_ref.shape[1]
    dn = d // n_split

    def gated(a):
        r = jnp.dot(a, w13_ref[...], preferred_element_type=F32)
        tf = r.shape[1] // 2
        gate = r[:, :tf]
        return (gate * jax.nn.sigmoid(gate) * r[:, tf:] * 0.5).astype(BF16)

    def start_tile():
        for rows in _slabs(h_s.shape[0]):
            h = h_s[rows, :]
            a = (h * _rms_scale(h) * g_ref[...]).astype(BF16)
            a_s[rows, :] = a
            p = gated(a)
            for s in range(n_split):
                cols = slice(s * dn, (s + 1) * dn)
                o_ref[rows, cols] = h_s[rows, cols] + jnp.dot(p, w2_ref[:, cols], preferred_element_type=F32)

    _row_tile_prefetch(h_hbm, h_s, sem, start_tile)

    @pl.when(pl.program_id(1) > 0)
    def _():
        p = gated(a_s[...])
        for s in range(n_split):
            cols = slice(s * dn, (s + 1) * dn)
            o_ref[:, cols] += jnp.dot(p, w2_ref[:, cols], preferred_element_type=F32)

    if final_norm:
        @pl.when(pl.program_id(1) == pl.num_programs(1) - 1)
        def _():
            for rows in _slabs(o_ref.shape[0]):
                hf = o_ref[rows, :]
                o_ref[rows, :] = hf * _rms_scale(hf) * gf_ref[...]


def _ffn(h, gain, weights, layer, t, final_gain, final_norm=False):
    seq, d = h.shape
    tm = t["tm"]
    w13, w2 = weights
    nf, tf = w13.shape[1], w13.shape[3] // 2
    assert nf >= 2
    return pl.pallas_call(
        functools.partial(_ffn_kernel, n_split=max(1, d // 1024), final_norm=final_norm),
        out_shape=jax.ShapeDtypeStruct((seq, d), F32),
        grid=(seq // tm, nf),
        in_specs=[
            pl.BlockSpec(memory_space=pl.ANY),
            pl.BlockSpec((None, 1, d), lambda i, f: (layer, 0, 0)),
            pl.BlockSpec((None, None, d, 2 * tf), lambda i, f: (layer, f, 0, 0)),
            pl.BlockSpec((None, tf, d), lambda i, f: (layer, f, 0)),
            pl.BlockSpec((1, d), lambda i, f: (0, 0)),
        ],
        out_specs=pl.BlockSpec((tm, d), lambda i, f: (i, 0)),
        scratch_shapes=[pltpu.VMEM((tm, d), BF16), pltpu.VMEM((tm, d), F32), pltpu.SemaphoreType.DMA(())],
        compiler_params=_cparams("arbitrary", "arbitrary"),
        name="ffn",
    )(h, gain, w13, w2, final_gain)


def _inproj_kernel(h_hbm, g_ref, w_ref, z_ref, a_s, h_s, sem):
    def start_tile():
        for rows in _slabs(h_s.shape[0]):
            h = h_s[rows, :]
            a = (h * _rms_scale(h) * g_ref[...]).astype(BF16)
            a_s[rows, :] = a
            z_ref[rows, :] = jnp.dot(a, w_ref[...], preferred_element_type=F32)

    _row_tile_prefetch(h_hbm, h_s, sem, start_tile)

    @pl.when(pl.program_id(1) > 0)
    def _():
        z_ref[...] = jnp.dot(a_s[...], w_ref[...], preferred_element_type=F32)


def _inproj(h, gain, w_in, layer, t):
    seq, d = h.shape
    nblk, tn = w_in.shape[1], w_in.shape[3]
    n = nblk * tn
    tm = t["tm_proj"]
    assert nblk >= 2
    return pl.pallas_call(
        _inproj_kernel,
        out_shape=jax.ShapeDtypeStruct((seq, n), F32),
        grid=(seq // tm, nblk),
        in_specs=[
            pl.BlockSpec(memory_space=pl.ANY),
            pl.BlockSpec((None, 1, d), lambda i, j: (layer, 0, 0)),
            pl.BlockSpec((None, None, d, tn), lambda i, j: (layer, j, 0, 0)),
        ],
        out_specs=pl.BlockSpec((tm, tn), lambda i, j: (i, j)),
        scratch_shapes=[pltpu.VMEM((tm, d), BF16), pltpu.VMEM((tm, d), F32), pltpu.SemaphoreType.DMA(())],
        compiler_params=_cparams("arbitrary", "arbitrary"),
        name="inproj",
    )(h, gain, w_in)


def _ret_kernel(q_ref, k_ref, v_ref, g_ref, cos_ref, sin_ref, dm_ref, qd_ref, kd_ref, cd_ref, gain_ref,
                o_ref, s_s, *, k_scale):
    @pl.when(pl.program_id(0) == 0)
    def _():
        s_s[...] = jnp.zeros_like(s_s)

    nh, hd = s_s.shape[0], s_s.shape[1]
    half = hd // 2
    cos = cos_ref[...]
    sin = sin_ref[...]

    def rope(t):
        te, to = t[:, :half], t[:, half:]
        return jnp.concatenate([te * cos - to * sin, to * cos + te * sin], axis=1)

    for h in range(nh):
        cols = slice(h * hd, (h + 1) * hd)
        qr = rope(q_ref[:, cols])
        kr = rope(k_ref[:, cols]) * k_scale
        qb = qr.astype(BF16)
        kb = kr.astype(BF16)
        vb = v_ref[:, cols].astype(BF16)
        scores = lax.dot_general(qb, kb, (((1,), (1,)), ((), ())), preferred_element_type=F32)
        scores = (scores * dm_ref[h]).astype(BF16)
        inner = jnp.dot(scores, vb, preferred_element_type=F32)
        state = s_s[h]
        qd = qd_ref[h]
        kd = kd_ref[h]
        cross = (jnp.dot(qb, state.astype(BF16), preferred_element_type=F32)
                 * jnp.concatenate([qd, qd], axis=1))
        kdec = (kr * jnp.concatenate([kd, kd], axis=1)).astype(BF16)
        s_s[h] = state * cd_ref[h] + lax.dot_general(kdec, vb, (((0,), (0,)), ((), ())),
                                                     preferred_element_type=F32)
        out = inner + cross
        out = out * _rms_scale(out) * gain_ref[:, cols]
        g = g_ref[:, cols]
        o_ref[:, cols] = (out * (g * jax.nn.sigmoid(g))).astype(o_ref.dtype)


def _retention(z, cos, sin, tabs, gain, layer, rw, t):
    seq = z.shape[0]
    nh = RET_HEADS
    hd = rw // nh
    c = t["ret_chunk"]
    dm, qd, kd, cd = tabs

    def zpart(part):
        return pl.BlockSpec((c, rw), lambda i: (i, part))

    def whole(a):
        return pl.BlockSpec(a.shape, lambda i: (0,) * a.ndim)

    return pl.pallas_call(
        functools.partial(_ret_kernel, k_scale=float(hd) ** -0.5),
        out_shape=jax.ShapeDtypeStruct((seq, rw), BF16),
        grid=(seq // c,),
        in_specs=[
            zpart(0), zpart(1), zpart(2), zpart(3),
            pl.BlockSpec((c, hd // 2), lambda i: (i, 0)),
            pl.BlockSpec((c, hd // 2), lambda i: (i, 0)),
            whole(dm), whole(qd), whole(kd),
            pl.BlockSpec(memory_space=pltpu.SMEM),
            pl.BlockSpec((None, 1, rw), lambda i: (layer, 0, 0)),
        ],
        out_specs=pl.BlockSpec((c, rw), lambda i: (i, 0)),
        scratch_shapes=[pltpu.VMEM((nh, hd, hd), F32)],
        compiler_params=_cparams("arbitrary"),
        name="retention",
    )(z, z, z, z, cos, sin, dm, qd, kd, cd, gain)


def _retention_tables(c, hd):
    lg = jnp.log(1.0 - 2.0 ** (-5.0 - jnp.arange(RET_HEADS, dtype=F32)))
    n = jnp.arange(c, dtype=F32)
    rel = n[:, None] - n[None, :]
    dm = jnp.where(rel >= 0, jnp.exp(lg[:, None, None] * jnp.maximum(rel, 0.0)), 0.0)
    qd = jnp.broadcast_to(jnp.exp(lg[:, None] * (n + 1.0))[..., None], (RET_HEADS, c, hd // 2))
    kd = jnp.broadcast_to(jnp.exp(lg[:, None] * (c - 1.0 - n))[..., None], (RET_HEADS, c, hd // 2))
    cd = jnp.exp(lg * c)
    return dm, qd, kd, cd


def _rope_tables(seq, hd):
    angle = 1.0 / (ROPE_BASE ** jnp.linspace(0.0, 1.0, hd // 2, dtype=F32))
    ang = jnp.arange(seq, dtype=F32)[:, None] * angle[None, :]
    return jnp.cos(ang), jnp.sin(ang)


def _s5_prep_kernel(are_ref, aim_ref, ls_ref, are_t, aim_t, ls_t, br_ref, bi_ref, wb_ref, lam_ref):
    ns = are_ref.shape[-1]
    w, p = br_ref.shape
    gb = ns // p
    row_group = lax.broadcasted_iota(jnp.int32, (w, ns), 0) // (w // gb)
    col_group = lax.broadcasted_iota(jnp.int32, (w, ns), 1) // p
    on_diag = row_group == col_group

    def lam_bar(ar, ai, ls):
        delta = jnp.exp(ls)
        mag = jnp.exp(ar * delta)
        return mag * jnp.cos(ai * delta), mag * jnp.sin(ai * delta)

    ar = are_ref[...]
    ai = aim_ref[...]
    lr, li = lam_bar(ar, ai, ls_ref[...])
    nr, ni = lr - 1.0, li
    inv = 1.0 / (ar * ar + ai * ai)
    kr = (nr * ar + ni * ai) * inv
    ki = (ni * ar - nr * ai) * inv
    br = pltpu.repeat(br_ref[...], gb, axis=1)
    bi = pltpu.repeat(bi_ref[...], gb, axis=1)
    wb_ref[:, :ns] = jnp.where(on_diag, kr * br - ki * bi, 0.0).astype(BF16)
    wb_ref[:, ns:] = jnp.where(on_diag, kr * bi + ki * br, 0.0).astype(BF16)
    tr, ti = lam_bar(are_t[...], aim_t[...], ls_t[...])
    lam_ref[0] = tr
    lam_ref[1] = ti


def _s5_prep(prm):
    are, aim, ls, are_t, aim_t, ls_t, wbr, wbi = prm
    depth, nb, w, p = wbr.shape
    ns = are.shape[-1]
    nst = are_t.shape[2]

    def blk(*shape):
        return pl.BlockSpec((None, None) + shape, lambda l, j: (l, j) + (0,) * len(shape))

    return pl.pallas_call(
        _s5_prep_kernel,
        out_shape=(jax.ShapeDtypeStruct((depth, nb, w, 2 * ns), BF16),
                   jax.ShapeDtypeStruct((depth, nb, 2, nst, V7X_LANES), F32)),
        grid=(depth, nb),
        in_specs=[blk(1, ns), blk(1, ns), blk(1, ns),
                  blk(nst, V7X_LANES), blk(nst, V7X_LANES), blk(nst, V7X_LANES),
                  blk(w, p), blk(w, p)],
        out_specs=(blk(w, 2 * ns), blk(2, nst, V7X_LANES)),
        compiler_params=_cparams("parallel", "parallel"),
        name="s5_prep",
    )(are, aim, ls, are_t, aim_t, ls_t, wbr, wbi)


def _s5_kernel(u_ref, wb_ref, lam_ref, wcr_ref, wci_ref, d_ref, wg_ref, bg_ref, gs_ref, y_ref,
               x_s, hx_s, h_s, y_s):
    nb, w = wb_ref.shape[0], wb_ref.shape[1]
    ns = wcr_ref.shape[1]
    nst = ns // V7X_LANES
    rows = u_ref.shape[0]
    pitch = x_s.shape[1] // nst
    group = x_s.shape[0] // 2

    @pl.when(pl.program_id(0) == 0)
    def _():
        h_s[...] = jnp.zeros_like(h_s)

    for first in range(0, nb, group):
        blocks = range(first, first + group)
        for k, jb in enumerate(blocks):
            cols = slice(jb * w, (jb + 1) * w)
            bu = jnp.dot(u_ref[:, cols].astype(BF16), wb_ref[jb], preferred_element_type=F32)
            for part in range(2):
                for s in range(nst):
                    lo = part * ns + s * V7X_LANES
                    x_s[2 * k + part, s * pitch:s * pitch + rows, :] = bu[:, lo:lo + V7X_LANES]

        lam = [(lam_ref[jb, 0], lam_ref[jb, 1]) for jb in blocks]

        def step(t, carry):
            out = []
            now = pl.ds(t, nst, stride=pitch)
            here = pl.ds(pl.multiple_of(t * nst, nst), nst)
            for k in range(group):
                hr, hi = carry[k]
                lr, li = lam[k]
                nr = (lr * hr - li * hi) + x_s[2 * k, now, :]
                ni = (lr * hi + li * hr) + x_s[2 * k + 1, now, :]
                hx_s[2 * k, here, :] = nr
                hx_s[2 * k + 1, here, :] = ni
                out.append((nr, ni))
            return tuple(out)

        init = tuple((h_s[2 * jb], h_s[2 * jb + 1]) for jb in blocks)
        fin = lax.fori_loop(0, rows, step, init, unroll=8)
        for k, jb in enumerate(blocks):
            h_s[2 * jb] = fin[k][0]
            h_s[2 * jb + 1] = fin[k][1]

        for k, jb in enumerate(blocks):
            def states(part):
                return jnp.concatenate(
                    [hx_s[2 * k + part, pl.ds(s, rows, stride=nst), :] for s in range(nst)], axis=1).astype(BF16)

            y = (jnp.dot(states(0), wcr_ref[jb], preferred_element_type=F32)
                 - jnp.dot(states(1), wci_ref[jb], preferred_element_type=F32))
            cols = slice(jb * w, (jb + 1) * w)
            y_s[:, cols] = y + d_ref[:, cols] * u_ref[:, cols]

    y = y_s[...]
    zg = jnp.dot(jax.nn.gelu(y).astype(BF16), wg_ref[...], preferred_element_type=F32) + bg_ref[...]
    y = y * jax.nn.sigmoid(zg)
    y_ref[...] = (y * _rms_scale(y) * gs_ref[...]).astype(y_ref.dtype)


def _s5(z, wb, lam, wcr, wci, dsk, w_glu, b_glu, g_s5, layer, col0, sw, t):
    seq = z.shape[0]
    nb, w, ns2 = wb.shape[1], wb.shape[2], wb.shape[3]
    ns = ns2 // 2
    nst = ns // V7X_LANES
    rows = t["s5_rows"]
    assert col0 % sw == 0
    pitch = rows + V7X_SUBLANES
    group = math.gcd(nb, t["s5_group"])

    return pl.pallas_call(
        _s5_kernel,
        out_shape=jax.ShapeDtypeStruct((seq, sw), BF16),
        grid=(seq // rows,),
        in_specs=[
            pl.BlockSpec((rows, sw), lambda i: (i, col0 // sw)),
            _resident((None, nb, w, ns2), lambda i: (layer, 0, 0, 0)),
            _resident((None, nb, 2, nst, V7X_LANES), lambda i: (layer, 0, 0, 0, 0)),
            _resident((None, nb, ns, w), lambda i: (layer, 0, 0, 0)),
            _resident((None, nb, ns, w), lambda i: (layer, 0, 0, 0)),
            _resident((None, 1, sw), lambda i: (layer, 0, 0)),
            _resident((None, sw, sw), lambda i: (layer, 0, 0)),
            _resident((None, 1, sw), lambda i: (layer, 0, 0)),
            _resident((None, 1, sw), lambda i: (layer, 0, 0)),
        ],
        out_specs=pl.BlockSpec((rows, sw), lambda i: (i, 0)),
        scratch_shapes=[
            pltpu.VMEM((2 * group, nst * pitch, V7X_LANES), F32),
            pltpu.VMEM((2 * group, nst * rows, V7X_LANES), F32),
            pltpu.VMEM((2 * nb, nst, V7X_LANES), F32),
            pltpu.VMEM((rows, sw), F32),
        ],
        compiler_params=_cparams("arbitrary"),
        name="s5",
    )(z, wb, lam, wcr, wci, dsk, w_glu, b_glu, g_s5)


def _s5_layout(a_re, a_im, b_re, b_im, c_re, c_im, log_step):
    depth, groups, p = a_re.shape
    ch = b_re.shape[-1]
    gb = V7X_MXU_DIM // ch
    nb = groups // gb
    ns = gb * p
    eye = jnp.eye(gb, dtype=F32)

    def lane(v):
        return v.astype(F32).reshape(depth, nb, 1, ns)

    def tile(v):
        return v.astype(F32).reshape(depth, nb, ns // V7X_LANES, V7X_LANES)

    def rows_b(b):
        return jnp.swapaxes(b.astype(F32).reshape(depth, nb, gb, p, ch), -1, -2).reshape(depth, nb, gb * ch, p)

    def bdiag_c(c):
        c5 = c.astype(F32).reshape(depth, nb, gb, ch, p)
        return jnp.einsum("ljgcp,gh->ljgphc", c5, eye).reshape(depth, nb, ns, gb * ch).astype(BF16)

    ls = jnp.broadcast_to(log_step.astype(F32)[..., None], (depth, groups, p))
    prep = (lane(a_re), lane(a_im), lane(ls), tile(a_re), tile(a_im), tile(ls), rows_b(b_re), rows_b(b_im))
    return prep, bdiag_c(c_re), bdiag_c(c_im)


def _outproj_kernel(yr_ref, ys_ref, h_ref, wo_ref, o_ref):
    rw = yr_ref.shape[1]
    mix = (jnp.dot(yr_ref[...], wo_ref[:rw, :], preferred_element_type=F32)
           + jnp.dot(ys_ref[...], wo_ref[rw:, :], preferred_element_type=F32))
    o_ref[...] = h_ref[...] + mix


def _outproj(h, y_ret, y_ssm, w_out, layer, t):
    seq, d = h.shape
    rw, sw = y_ret.shape[1], y_ssm.shape[1]
    nblk, tn = w_out.shape[1], w_out.shape[3]
    tm = t["tm_proj"]
    return pl.pallas_call(
        _outproj_kernel,
        out_shape=jax.ShapeDtypeStruct((seq, d), F32),
        grid=(seq // tm, nblk),
        in_specs=[
            pl.BlockSpec((tm, rw), lambda i, j: (i, 0)),
            pl.BlockSpec((tm, sw), lambda i, j: (i, 0)),
            pl.BlockSpec((tm, tn), lambda i, j: (i, j)),
            pl.BlockSpec((None, None, rw + sw, tn), lambda i, j: (layer, j, 0, 0)),
        ],
        out_specs=pl.BlockSpec((tm, tn), lambda i, j: (i, j)),
        compiler_params=_cparams("parallel", "arbitrary"),
        name="outproj",
    )(y_ret, y_ssm, h, w_out)


def _cast_w13_kernel(w1_ref, w3_ref, o_ref, *, f_valid):
    tf = w1_ref.shape[1]
    col = pl.program_id(1) * tf + lax.broadcasted_iota(jnp.int32, (1, tf), 1)
    ok = col < f_valid
    o_ref[:, :tf] = jnp.where(ok, w1_ref[...], 0.0).astype(BF16)
    o_ref[:, tf:] = jnp.where(ok, w3_ref[...], 0.0).astype(BF16)


def _cast_w2_kernel(w2_ref, o_ref, *, f_valid):
    tf = w2_ref.shape[0]
    row = pl.program_id(1) * tf + lax.broadcasted_iota(jnp.int32, (tf, 1), 0)
    o_ref[...] = jnp.where(row < f_valid, w2_ref[...], 0.0).astype(BF16)


def _cast_block_kernel(w_ref, o_ref):
    o_ref[...] = w_ref[...].astype(BF16)


def _prep_ffn(w1, w3, w2, tf):
    depth, d, f = w1.shape
    nf = pl.cdiv(f, tf)
    w13 = pl.pallas_call(
        functools.partial(_cast_w13_kernel, f_valid=f),
        out_shape=jax.ShapeDtypeStruct((depth, nf, d, 2 * tf), BF16),
        grid=(depth, nf),
        in_specs=[pl.BlockSpec((None, d, tf), lambda l, j: (l, 0, j)),
                  pl.BlockSpec((None, d, tf), lambda l, j: (l, 0, j))],
        out_specs=pl.BlockSpec((None, None, d, 2 * tf), lambda l, j: (l, j, 0, 0)),
        compiler_params=_cparams("parallel", "parallel"),
        name="cast_w13",
    )(w1, w3)
    w2p = pl.pallas_call(
        functools.partial(_cast_w2_kernel, f_valid=f),
        out_shape=jax.ShapeDtypeStruct((depth, nf * tf, d), BF16),
        grid=(depth, nf),
        in_specs=[pl.BlockSpec((None, tf, d), lambda l, j: (l, j, 0))],
        out_specs=pl.BlockSpec((None, tf, d), lambda l, j: (l, j, 0)),
        compiler_params=_cparams("parallel", "parallel"),
        name="cast_w2",
    )(w2)
    return w13, w2p


def _cast_w_in_kernel(w_ref, p_ref, o_ref, *, n_perm):
    j = pl.program_id(1)

    @pl.when(j < n_perm)
    def _():
        o_ref[...] = jnp.dot(w_ref[...].astype(BF16), p_ref[...], preferred_element_type=F32).astype(BF16)

    @pl.when(j >= n_perm)
    def _():
        o_ref[...] = w_ref[...].astype(BF16)


def _prep_w_in(w_in, rw, tn):
    depth, d, n = w_in.shape
    hd = rw // RET_HEADS
    per = tn // hd
    src = jnp.arange(hd)
    dst = (src % 2) * (hd // 2) + src // 2
    perm = (dst[:, None] == jnp.arange(hd)[None, :]).astype(BF16)
    return pl.pallas_call(
        functools.partial(_cast_w_in_kernel, n_perm=2 * RET_HEADS),
        out_shape=jax.ShapeDtypeStruct((depth, n // tn, d, tn), BF16),
        grid=(depth, n // hd),
        in_specs=[pl.BlockSpec((None, d, hd), lambda l, j: (l, 0, j)),
                  pl.BlockSpec((hd, hd), lambda l, j: (0, 0))],
        out_specs=pl.BlockSpec((None, None, d, hd), lambda l, j: (l, j // per, 0, j % per)),
        compiler_params=_cparams("parallel", "parallel"),
        name="cast_w_in",
    )(w_in, perm)


def _prep_blocked(w, tn):
    depth, k, n = w.shape
    return pl.pallas_call(
        _cast_block_kernel,
        out_shape=jax.ShapeDtypeStruct((depth, n // tn, k, tn), BF16),
        grid=(depth, n // tn),
        in_specs=[pl.BlockSpec((None, k, tn), lambda l, j: (l, 0, j))],
        out_specs=pl.BlockSpec((None, None, k, tn), lambda l, j: (l, j, 0, 0)),
        compiler_params=_cparams("parallel", "parallel"),
        name="cast_blocked",
    )(w)


def kernel(x, ffn1_norm, ffn1_w1, ffn1_w3, ffn1_w2, mix_norm, w_in, s5_a_re, s5_a_im, s5_b_re, s5_b_im,
           s5_c_re, s5_c_im, s5_d, s5_log_step, s5_w_glu, s5_b_glu, s5_out_norm, ret_out_norm, w_out,
           ffn2_norm, ffn2_w1, ffn2_w3, ffn2_w2, final_norm):
    batch, seq, d = x.shape
    assert batch == 1, "sequence kernels carry state along the row axis of a single sequence"
    depth = w_in.shape[0]
    sw = s5_d.shape[-1]
    rw = ret_out_norm.shape[-1]
    assert w_in.shape[-1] == 4 * rw + sw and rw % V7X_MXU_DIM == 0
    hd = rw // RET_HEADS
    t = _tiles(seq, d, w_in.shape[-1])

    ffn_a = _prep_ffn(ffn1_w1, ffn1_w3, ffn1_w2, t["tf"])
    ffn_b = _prep_ffn(ffn2_w1, ffn2_w3, ffn2_w2, t["tf"])
    w_in_b = _prep_w_in(w_in, rw, t["tn_in"])
    w_glu_b = s5_w_glu.astype(BF16)
    w_out_b = _prep_blocked(w_out, t["tn_out"])
    s5_raw, wcr, wci = _s5_layout(s5_a_re, s5_a_im, s5_b_re, s5_b_im, s5_c_re, s5_c_im, s5_log_step)
    wb, lam = _s5_prep(s5_raw)
    cos, sin = _rope_tables(seq, hd)
    ret_tabs = _retention_tables(t["ret_chunk"], hd)

    def row3(v):
        return v.astype(F32)[:, None, :]

    g1, gm, g2 = row3(ffn1_norm), row3(mix_norm), row3(ffn2_norm)
    bg, gs, gr, dsk = row3(s5_b_glu), row3(s5_out_norm), row3(ret_out_norm), row3(s5_d)
    gf = final_norm.astype(F32)[None, :]

    h = x.reshape(seq, d)
    for l in range(depth):
        h = _ffn(h, g1, ffn_a, l, t, gf)
        z = _inproj(h, gm, w_in_b, l, t)
        y_ret = _retention(z, cos, sin, ret_tabs, gr, l, rw, t)
        y_ssm = _s5(z, wb, lam, wcr, wci, dsk, w_glu_b, bg, gs, l, 4 * rw, sw, t)
        h = _outproj(h, y_ret, y_ssm, w_out_b, l, t)
        h = _ffn(h, g2, ffn_b, l, t, gf, final_norm=(l == depth - 1))
    return h.reshape(batch, seq, d)
```

```python
import functools
import math

import jax
import jax.numpy as jnp
from jax import lax
from jax.experimental import pallas as pl
from jax.experimental.pallas import tpu as pltpu

F32 = jnp.float32
BF16 = jnp.bfloat16

NORM_EPS = 1e-6
RET_HEADS = 8
ROPE_BASE = 10000.0

V7X_SUBLANES = 8
V7X_LANES = 128
V7X_MXU_DIM = 256
V7X_VMEM_LIMIT_BYTES = 62 * 1024 * 1024


def _divisor_tile(n, want):
    for cand in range(want, 0, -V7X_MXU_DIM):
        if n % cand == 0:
            return cand
    raise ValueError(f"no tile <= {want} divides {n}")


def _tiles(seq, d, n_in):
    return dict(
        tm=_divisor_tile(seq, 512),
        tf=512,
        tm_proj=_divisor_tile(seq, 1024),
        tn_in=_divisor_tile(n_in, 1024),
        tn_out=_divisor_tile(d, 1024),
        ret_chunk=_divisor_tile(seq, 256),
        s5_rows=_divisor_tile(seq, 256),
        s5_group=4,
    )


NORM_SLAB_ROWS = 256


def _cparams(*sem):
    return pltpu.CompilerParams(dimension_semantics=sem, vmem_limit_bytes=V7X_VMEM_LIMIT_BYTES)


def _rms_scale(x):
    return lax.rsqrt(jnp.mean(x * x, axis=-1, keepdims=True) + NORM_EPS)


def _slabs(rows):
    step = min(NORM_SLAB_ROWS, rows)
    return [slice(r, r + step) for r in range(0, rows, step)]


def _resident(shape, index_map):
    return pl.BlockSpec(shape, index_map, pipeline_mode=pl.Buffered(1))


def _row_tile_prefetch(h_hbm, h_s, sem, consume):
    i, j = pl.program_id(0), pl.program_id(1)
    tm = h_s.shape[0]

    def tile_copy(tile):
        return pltpu.make_async_copy(h_hbm.at[pl.ds(pl.multiple_of(tile * tm, tm), tm), :], h_s, sem)

    @pl.when((i == 0) & (j == 0))
    def _():
        tile_copy(0).start()

    @pl.when(j == 0)
    def _():
        tile_copy(i).wait()
        consume()

    @pl.when((j == 1) & (i + 1 < pl.num_programs(0)))
    def _():
        tile_copy(i + 1).start()


def _ffn_kernel(h_hbm, g_ref, w13_ref, w2_ref, gf_ref, o_ref, a_s, h_s, sem, *, n_split, final_norm):
    d = o_ref.shape[1]
    dn = d // n_split

    def gated(a):
        r = jnp.dot(a, w13_ref[...], preferred_element_type=F32)
        tf = r.shape[1] // 2
        gate = r[:, :tf]
        return (gate * jax.nn.sigmoid(gate) * r[:, tf:] * 0.5).astype(BF16)

    def start_tile():
        for rows in _slabs(h_s.shape[0]):
            h = h_s[rows, :]
            a = (h * _rms_scale(h) * g_ref[...]).astype(BF16)
            a_s[rows, :] = a
            p = gated(a)
            for s in range(n_split):
                cols = slice(s * dn, (s + 1) * dn)
                o_ref[rows, cols] = h_s[rows, cols] + jnp.dot(p, w2_ref[:, cols], preferred_element_type=F32)

    _row_tile_prefetch(h_hbm, h_s, sem, start_tile)

    @pl.when(pl.program_id(1) > 0)
    def _():
        p = gated(a_s[...])
        for s in range(n_split):
            cols = slice(s * dn, (s + 1) * dn)
            o_ref[:, cols] += jnp.dot(p, w2_ref[:, cols], preferred_element_type=F32)

    if final_norm:
        @pl.when(pl.program_id(1) == pl.num_programs(1) - 1)
        def _():
            for rows in _slabs(o_ref.shape[0]):
                hf = o_ref[rows, :]
                o_ref[rows, :] = hf * _rms_scale(hf) * gf_ref[...]


def _ffn(h, gain, weights, layer, t, final_gain, final_norm=False):
    seq, d = h.shape
    tm = t["tm"]
    w13, w2 = weights
    nf, tf = w13.shape[1], w13.shape[3] // 2
    assert nf >= 2
    return pl.pallas_call(
        functools.partial(_ffn_kernel, n_split=max(1, d // 1024), final_norm=final_norm),
        out_shape=jax.ShapeDtypeStruct((seq, d), F32),
        grid=(seq // tm, nf),
        in_specs=[
            pl.BlockSpec(memory_space=pl.ANY),
            pl.BlockSpec((None, 1, d), lambda i, f: (layer, 0, 0)),
            pl.BlockSpec((None, None, d, 2 * tf), lambda i, f: (layer, f, 0, 0)),
            pl.BlockSpec((None, tf, d), lambda i, f: (layer, f, 0)),
            pl.BlockSpec((1, d), lambda i, f: (0, 0)),
        ],
        out_specs=pl.BlockSpec((tm, d), lambda i, f: (i, 0)),
        scratch_shapes=[pltpu.VMEM((tm, d), BF16), pltpu.VMEM((tm, d), F32), pltpu.SemaphoreType.DMA(())],
        compiler_params=_cparams("arbitrary", "arbitrary"),
        name="ffn",
    )(h, gain, w13, w2, final_gain)


def _inproj_kernel(h_hbm, g_ref, w_ref, z_ref, a_s, h_s, sem):
    def start_tile():
        for rows in _slabs(h_s.shape[0]):
            h = h_s[rows, :]
            a = (h * _rms_scale(h) * g_ref[...]).astype(BF16)
            a_s[rows, :] = a
            z_ref[rows, :] = jnp.dot(a, w_ref[...], preferred_element_type=F32)

    _row_tile_prefetch(h_hbm, h_s, sem, start_tile)

    @pl.when(pl.program_id(1) > 0)
    def _():
        z_ref[...] = jnp.dot(a_s[...], w_ref[...], preferred_element_type=F32)


def _inproj(h, gain, w_in, layer, t):
    seq, d = h.shape
    nblk, tn = w_in.shape[1], w_in.shape[3]
    n = nblk * tn
    tm = t["tm_proj"]
    assert nblk >= 2
    return pl.pallas_call(
        _inproj_kernel,
        out_shape=jax.ShapeDtypeStruct((seq, n), F32),
        grid=(seq // tm, nblk),
        in_specs=[
            pl.BlockSpec(memory_space=pl.ANY),
            pl.BlockSpec((None, 1, d), lambda i, j: (layer, 0, 0)),
            pl.BlockSpec((None, None, d, tn), lambda i, j: (layer, j, 0, 0)),
        ],
        out_specs=pl.BlockSpec((tm, tn), lambda i, j: (i, j)),
        scratch_shapes=[pltpu.VMEM((tm, d), BF16), pltpu.VMEM((tm, d), F32), pltpu.SemaphoreType.DMA(())],
        compiler_params=_cparams("arbitrary", "arbitrary"),
        name="inproj",
    )(h, gain, w_in)


def _ret_kernel(q_ref, k_ref, v_ref, g_ref, cos_ref, sin_ref, dm_ref, qd_ref, kd_ref, cd_ref, gain_ref,
                o_ref, s_s, *, k_scale):
    @pl.when(pl.program_id(0) == 0)
    def _():
        s_s[...] = jnp.zeros_like(s_s)

    nh, hd = s_s.shape[0], s_s.shape[1]
    half = hd // 2
    cos = cos_ref[...]
    sin = sin_ref[...]

    def rope(t):
        te, to = t[:, :half], t[:, half:]
        return jnp.concatenate([te * cos - to * sin, to * cos + te * sin], axis=1)

    for h in range(nh):
        cols = slice(h * hd, (h + 1) * hd)
        qr = rope(q_ref[:, cols])
        kr = rope(k_ref[:, cols]) * k_scale
        qb = qr.astype(BF16)
        kb = kr.astype(BF16)
        vb = v_ref[:, cols].astype(BF16)
        scores = lax.dot_general(qb, kb, (((1,), (1,)), ((), ())), preferred_element_type=F32)
        scores = (scores * dm_ref[h]).astype(BF16)
        inner = jnp.dot(scores, vb, preferred_element_type=F32)
        state = s_s[h]
        qd = qd_ref[h]
        kd = kd_ref[h]
        cross = (jnp.dot(qb, state.astype(BF16), preferred_element_type=F32)
                 * jnp.concatenate([qd, qd], axis=1))
        kdec = (kr * jnp.concatenate([kd, kd], axis=1)).astype(BF16)
        s_s[h] = state * cd_ref[h] + lax.dot_general(kdec, vb, (((0,), (0,)), ((), ())),
                                                     preferred_element_type=F32)
        out = inner + cross
        out = out * _rms_scale(out) * gain_ref[:, cols]
        g = g_ref[:, cols]
        o_ref[:, cols] = (out * (g * jax.nn.sigmoid(g))).astype(o_ref.dtype)


def _retention(z, cos, sin, tabs, gain, layer, rw, t):
    seq = z.shape[0]
    nh = RET_HEADS
    hd = rw // nh
    c = t["ret_chunk"]
    dm, qd, kd, cd = tabs

    def zpart(part):
        return pl.BlockSpec((c, rw), lambda i: (i, part))

    def whole(a):
        return pl.BlockSpec(a.shape, lambda i: (0,) * a.ndim)

    return pl.pallas_call(
        functools.partial(_ret_kernel, k_scale=float(hd) ** -0.5),
        out_shape=jax.ShapeDtypeStruct((seq, rw), BF16),
        grid=(seq // c,),
        in_specs=[
            zpart(0), zpart(1), zpart(2), zpart(3),
            pl.BlockSpec((c, hd // 2), lambda i: (i, 0)),
            pl.BlockSpec((c, hd // 2), lambda i: (i, 0)),
            whole(dm), whole(qd), whole(kd),
            pl.BlockSpec(memory_space=pltpu.SMEM),
            pl.BlockSpec((None, 1, rw), lambda i: (layer, 0, 0)),
        ],
        out_specs=pl.BlockSpec((c, rw), lambda i: (i, 0)),
        scratch_shapes=[pltpu.VMEM((nh, hd, hd), F32)],
        compiler_params=_cparams("arbitrary"),
        name="retention",
    )(z, z, z, z, cos, sin, dm, qd, kd, cd, gain)


def _retention_tables(c, hd):
    lg = jnp.log(1.0 - 2.0 ** (-5.0 - jnp.arange(RET_HEADS, dtype=F32)))
    n = jnp.arange(c, dtype=F32)
    rel = n[:, None] - n[None, :]
    dm = jnp.where(rel >= 0, jnp.exp(lg[:, None, None] * jnp.maximum(rel, 0.0)), 0.0)
    qd = jnp.broadcast_to(jnp.exp(lg[:, None] * (n + 1.0))[..., None], (RET_HEADS, c, hd // 2))
    kd = jnp.broadcast_to(jnp.exp(lg[:, None] * (c - 1.0 - n))[..., None], (RET_HEADS, c, hd // 2))
    cd = jnp.exp(lg * c)
    return dm, qd, kd, cd


def _rope_tables(seq, hd):
    angle = 1.0 / (ROPE_BASE ** jnp.linspace(0.0, 1.0, hd // 2, dtype=F32))
    ang = jnp.arange(seq, dtype=F32)[:, None] * angle[None, :]
    return jnp.cos(ang), jnp.sin(ang)


def _s5_prep_kernel(are_ref, aim_ref, ls_ref, are_t, aim_t, ls_t, br_ref, bi_ref, wb_ref, lam_ref):
    ns = are_ref.shape[-1]
    w, p = br_ref.shape
    gb = ns // p
    row_group = lax.broadcasted_iota(jnp.int32, (w, ns), 0) // (w // gb)
    col_group = lax.broadcasted_iota(jnp.int32, (w, ns), 1) // p
    on_diag = row_group == col_group

    def lam_bar(ar, ai, ls):
        delta = jnp.exp(ls)
        mag = jnp.exp(ar * delta)
        return mag * jnp.cos(ai * delta), mag * jnp.sin(ai * delta)

    ar = are_ref[...]
    ai = aim_ref[...]
    lr, li = lam_bar(ar, ai, ls_ref[...])
    nr, ni = lr - 1.0, li
    inv = 1.0 / (ar * ar + ai * ai)
    kr = (nr * ar + ni * ai) * inv
    ki = (ni * ar - nr * ai) * inv
    br = jnp.concatenate([br_ref[...]] * gb, axis=1)
    bi = jnp.concatenate([bi_ref[...]] * gb, axis=1)
    wb_ref[:, :ns] = jnp.where(on_diag, kr * br - ki * bi, 0.0).astype(BF16)
    wb_ref[:, ns:] = jnp.where(on_diag, kr * bi + ki * br, 0.0).astype(BF16)
    tr, ti = lam_bar(are_t[...], aim_t[...], ls_t[...])
    lam_ref[0] = tr
    lam_ref[1] = ti


def _s5_prep(prm):
    are, aim, ls, are_t, aim_t, ls_t, wbr, wbi = prm
    depth, nb, w, p = wbr.shape
    ns = are.shape[-1]
    nst = are_t.shape[2]

    def blk(*shape):
        return pl.BlockSpec((None, None) + shape, lambda l, j: (l, j) + (0,) * len(shape))

    return pl.pallas_call(
        _s5_prep_kernel,
        out_shape=(jax.ShapeDtypeStruct((depth, nb, w, 2 * ns), BF16),
                   jax.ShapeDtypeStruct((depth, nb, 2, nst, V7X_LANES), F32)),
        grid=(depth, nb),
        in_specs=[blk(1, ns), blk(1, ns), blk(1, ns),
                  blk(nst, V7X_LANES), blk(nst, V7X_LANES), blk(nst, V7X_LANES),
                  blk(w, p), blk(w, p)],
        out_specs=(blk(w, 2 * ns), blk(2, nst, V7X_LANES)),
        compiler_params=_cparams("parallel", "parallel"),
        name="s5_prep",
    )(are, aim, ls, are_t, aim_t, ls_t, wbr, wbi)


def _s5_kernel(u_ref, wb_ref, lam_ref, wcr_ref, wci_ref, d_ref, wg_ref, bg_ref, gs_ref, y_ref,
               x_s, hx_s, h_s, y_s):
    nb, w = wb_ref.shape[0], wb_ref.shape[1]
    ns = wcr_ref.shape[1]
    nst = ns // V7X_LANES
    rows = u_ref.shape[0]
    pitch = x_s.shape[1] // nst
    group = x_s.shape[0] // 2

    @pl.when(pl.program_id(0) == 0)
    def _():
        h_s[...] = jnp.zeros_like(h_s)

    for first in range(0, nb, group):
        blocks = range(first, first + group)
        for k, jb in enumerate(blocks):
            cols = slice(jb * w, (jb + 1) * w)
            bu = jnp.dot(u_ref[:, cols].astype(BF16), wb_ref[jb], preferred_element_type=F32)
            for part in range(2):
                for s in range(nst):
                    lo = part * ns + s * V7X_LANES
                    x_s[2 * k + part, s * pitch:s * pitch + rows, :] = bu[:, lo:lo + V7X_LANES]

        lam = [(lam_ref[jb, 0], lam_ref[jb, 1]) for jb in blocks]

        def step(t, carry):
            out = []
            now = pl.ds(t, nst, stride=pitch)
            here = pl.ds(pl.multiple_of(t * nst, nst), nst)
            for k in range(group):
                hr, hi = carry[k]
                lr, li = lam[k]
                nr = (lr * hr - li * hi) + x_s[2 * k, now, :]
                ni = (lr * hi + li * hr) + x_s[2 * k + 1, now, :]
                hx_s[2 * k, here, :] = nr
                hx_s[2 * k + 1, here, :] = ni
                out.append((nr, ni))
            return tuple(out)

        init = tuple((h_s[2 * jb], h_s[2 * jb + 1]) for jb in blocks)
        fin = lax.fori_loop(0, rows, step, init, unroll=8)
        for k, jb in enumerate(blocks):
            h_s[2 * jb] = fin[k][0]
            h_s[2 * jb + 1] = fin[k][1]

        for k, jb in enumerate(blocks):
            def states(part):
                return jnp.concatenate(
                    [hx_s[2 * k + part, pl.ds(s, rows, stride=nst), :] for s in range(nst)], axis=1).astype(BF16)

            y = (jnp.dot(states(0), wcr_ref[jb], preferred_element_type=F32)
                 - jnp.dot(states(1), wci_ref[jb], preferred_element_type=F32))
            cols = slice(jb * w, (jb + 1) * w)
            y_s[:, cols] = y + d_ref[:, cols] * u_ref[:, cols]

    y = y_s[...]
    zg = jnp.dot(jax.nn.gelu(y).astype(BF16), wg_ref[...], preferred_element_type=F32) + bg_ref[...]
    y = y * jax.nn.sigmoid(zg)
    y_ref[...] = (y * _rms_scale(y) * gs_ref[...]).astype(y_ref.dtype)


def _s5(z, wb, lam, wcr, wci, dsk, w_glu, b_glu, g_s5, layer, col0, sw, t):
    seq = z.shape[0]
    nb, w, ns2 = wb.shape[1], wb.shape[2], wb.shape[3]
    ns = ns2 // 2
    nst = ns // V7X_LANES
    rows = t["s5_rows"]
    assert col0 % sw == 0
    pitch = rows + V7X_SUBLANES
    group = math.gcd(nb, t["s5_group"])

    return pl.pallas_call(
        _s5_kernel,
        out_shape=jax.ShapeDtypeStruct((seq, sw), BF16),
        grid=(seq // rows,),
        in_specs=[
            pl.BlockSpec((rows, sw), lambda i: (i, col0 // sw)),
            _resident((None, nb, w, ns2), lambda i: (layer, 0, 0, 0)),
            _resident((None, nb, 2, nst, V7X_LANES), lambda i: (layer, 0, 0, 0, 0)),
            _resident((None, nb, ns, w), lambda i: (layer, 0, 0, 0)),
            _resident((None, nb, ns, w), lambda i: (layer, 0, 0, 0)),
            _resident((None, 1, sw), lambda i: (layer, 0, 0)),
            _resident((None, sw, sw), lambda i: (layer, 0, 0)),
            _resident((None, 1, sw), lambda i: (layer, 0, 0)),
            _resident((None, 1, sw), lambda i: (layer, 0, 0)),
        ],
        out_specs=pl.BlockSpec((rows, sw), lambda i: (i, 0)),
        scratch_shapes=[
            pltpu.VMEM((2 * group, nst * pitch, V7X_LANES), F32),
            pltpu.VMEM((2 * group, nst * rows, V7X_LANES), F32),
            pltpu.VMEM((2 * nb, nst, V7X_LANES), F32),
            pltpu.VMEM((rows, sw), F32),
        ],
        compiler_params=_cparams("arbitrary"),
        name="s5",
    )(z, wb, lam, wcr, wci, dsk, w_glu, b_glu, g_s5)


def _s5_layout(a_re, a_im, b_re, b_im, c_re, c_im, log_step):
    depth, groups, p = a_re.shape
    ch = b_re.shape[-1]
    gb = V7X_MXU_DIM // ch
    nb = groups // gb
    ns = gb * p
    eye = jnp.eye(gb, dtype=F32)

    def lane(v):
        return v.astype(F32).reshape(depth, nb, 1, ns)

    def tile(v):
        return v.astype(F32).reshape(depth, nb, ns // V7X_LANES, V7X_LANES)

    def rows_b(b):
        return jnp.swapaxes(b.astype(F32).reshape(depth, nb, gb, p, ch), -1, -2).reshape(depth, nb, gb * ch, p)

    def bdiag_c(c):
        c5 = c.astype(F32).reshape(depth, nb, gb, ch, p)
        return jnp.einsum("ljgcp,gh->ljgphc", c5, eye).reshape(depth, nb, ns, gb * ch).astype(BF16)

    ls = jnp.broadcast_to(log_step.astype(F32)[..., None], (depth, groups, p))
    prep = (lane(a_re), lane(a_im), lane(ls), tile(a_re), tile(a_im), tile(ls), rows_b(b_re), rows_b(b_im))
    return prep, bdiag_c(c_re), bdiag_c(c_im)


def _outproj_kernel(yr_ref, ys_ref, h_ref, wo_ref, o_ref):
    rw = yr_ref.shape[1]
    mix = (jnp.dot(yr_ref[...], wo_ref[:rw, :], preferred_element_type=F32)
           + jnp.dot(ys_ref[...], wo_ref[rw:, :], preferred_element_type=F32))
    o_ref[...] = h_ref[...] + mix


def _outproj(h, y_ret, y_ssm, w_out, layer, t):
    seq, d = h.shape
    rw, sw = y_ret.shape[1], y_ssm.shape[1]
    nblk, tn = w_out.shape[1], w_out.shape[3]
    tm = t["tm_proj"]
    return pl.pallas_call(
        _outproj_kernel,
        out_shape=jax.ShapeDtypeStruct((seq, d), F32),
        grid=(seq // tm, nblk),
        in_specs=[
            pl.BlockSpec((tm, rw), lambda i, j: (i, 0)),
            pl.BlockSpec((tm, sw), lambda i, j: (i, 0)),
            pl.BlockSpec((tm, tn), lambda i, j: (i, j)),
            pl.BlockSpec((None, None, rw + sw, tn), lambda i, j: (layer, j, 0, 0)),
        ],
        out_specs=pl.BlockSpec((tm, tn), lambda i, j: (i, j)),
        compiler_params=_cparams("parallel", "arbitrary"),
        name="outproj",
    )(y_ret, y_ssm, h, w_out)


def _cast_w13_kernel(w1_ref, w3_ref, o_ref, *, f_valid):
    tf = w1_ref.shape[1]
    col = pl.program_id(1) * tf + lax.broadcasted_iota(jnp.int32, (1, tf), 1)
    ok = col < f_valid
    o_ref[:, :tf] = jnp.where(ok, w1_ref[...], 0.0).astype(BF16)
    o_ref[:, tf:] = jnp.where(ok, w3_ref[...], 0.0).astype(BF16)


def _cast_w2_kernel(w2_ref, o_ref, *, f_valid):
    tf = w2_ref.shape[0]
    row = pl.program_id(1) * tf + lax.broadcasted_iota(jnp.int32, (tf, 1), 0)
    o_ref[...] = jnp.where(row < f_valid, w2_ref[...], 0.0).astype(BF16)


def _cast_block_kernel(w_ref, o_ref):
    o_ref[...] = w_ref[...].astype(BF16)


def _prep_ffn(w1, w3, w2, tf):
    depth, d, f = w1.shape
    nf = pl.cdiv(f, tf)
    w13 = pl.pallas_call(
        functools.partial(_cast_w13_kernel, f_valid=f),
        out_shape=jax.ShapeDtypeStruct((depth, nf, d, 2 * tf), BF16),
        grid=(depth, nf),
        in_specs=[pl.BlockSpec((None, d, tf), lambda l, j: (l, 0, j)),
                  pl.BlockSpec((None, d, tf), lambda l, j: (l, 0, j))],
        out_specs=pl.BlockSpec((None, None, d, 2 * tf), lambda l, j: (l, j, 0, 0)),
        compiler_params=_cparams("parallel", "parallel"),
        name="cast_w13",
    )(w1, w3)
    w2p = pl.pallas_call(
        functools.partial(_cast_w2_kernel, f_valid=f),
        out_shape=jax.ShapeDtypeStruct((depth, nf * tf, d), BF16),
        grid=(depth, nf),
        in_specs=[pl.BlockSpec((None, tf, d), lambda l, j: (l, j, 0))],
        out_specs=pl.BlockSpec((None, tf, d), lambda l, j: (l, j, 0)),
        compiler_params=_cparams("parallel", "parallel"),
        name="cast_w2",
    )(w2)
    return w13, w2p


def _cast_w_in_kernel(w_ref, p_ref, o_ref, *, n_perm):
    j = pl.program_id(1)

    @pl.when(j < n_perm)
    def _():
        o_ref[...] = jnp.dot(w_ref[...].astype(BF16), p_ref[...], preferred_element_type=F32).astype(BF16)

    @pl.when(j >= n_perm)
    def _():
        o_ref[...] = w_ref[...].astype(BF16)


def _prep_w_in(w_in, rw, tn):
    depth, d, n = w_in.shape
    hd = rw // RET_HEADS
    per = tn // hd
    src = jnp.arange(hd)
    dst = (src % 2) * (hd // 2) + src // 2
    perm = (dst[:, None] == jnp.arange(hd)[None, :]).astype(BF16)
    return pl.pallas_call(
        functools.partial(_cast_w_in_kernel, n_perm=2 * RET_HEADS),
        out_shape=jax.ShapeDtypeStruct((depth, n // tn, d, tn), BF16),
        grid=(depth, n // hd),
        in_specs=[pl.BlockSpec((None, d, hd), lambda l, j: (l, 0, j)),
                  pl.BlockSpec((hd, hd), lambda l, j: (0, 0))],
        out_specs=pl.BlockSpec((None, None, d, hd), lambda l, j: (l, j // per, 0, j % per)),
        compiler_params=_cparams("parallel", "parallel"),
        name="cast_w_in",
    )(w_in, perm)


def _prep_blocked(w, tn):
    depth, k, n = w.shape
    return pl.pallas_call(
        _cast_block_kernel,
        out_shape=jax.ShapeDtypeStruct((depth, n // tn, k, tn), BF16),
        grid=(depth, n // tn),
        in_specs=[pl.BlockSpec((None, k, tn), lambda l, j: (l, 0, j))],
        out_specs=pl.BlockSpec((None, None, k, tn), lambda l, j: (l, j, 0, 0)),
        compiler_params=_cparams("parallel", "parallel"),
        name="cast_blocked",
    )(w)


def kernel(x, ffn1_norm, ffn1_w1, ffn1_w3, ffn1_w2, mix_norm, w_in, s5_a_re, s5_a_im, s5_b_re, s5_b_im,
           s5_c_re, s5_c_im, s5_d, s5_log_step, s5_w_glu, s5_b_glu, s5_out_norm, ret_out_norm, w_out,
           ffn2_norm, ffn2_w1, ffn2_w3, ffn2_w2, final_norm):
    batch, seq, d = x.shape
    assert batch == 1, "sequence kernels carry state along the row axis of a single sequence"
    depth = w_in.shape[0]
    sw = s5_d.shape[-1]
    rw = ret_out_norm.shape[-1]
    assert w_in.shape[-1] == 4 * rw + sw and rw % V7X_MXU_DIM == 0
    hd = rw // RET_HEADS
    t = _tiles(seq, d, w_in.shape[-1])

    ffn_a = _prep_ffn(ffn1_w1, ffn1_w3, ffn1_w2, t["tf"])
    ffn_b = _prep_ffn(ffn2_w1, ffn2_w3, ffn2_w2, t["tf"])
    w_in_b = _prep_w_in(w_in, rw, t["tn_in"])
    w_glu_b = s5_w_glu.astype(BF16)
    w_out_b = _prep_blocked(w_out, t["tn_out"])
    s5_raw, wcr, wci = _s5_layout(s5_a_re, s5_a_im, s5_b_re, s5_b_im, s5_c_re, s5_c_im, s5_log_step)
    wb, lam = _s5_prep(s5_raw)
    cos, sin = _rope_tables(seq, hd)
    ret_tabs = _retention_tables(t["ret_chunk"], hd)

    def row3(v):
        return v.astype(F32)[:, None, :]

    g1, gm, g2 = row3(ffn1_norm), row3(mix_norm), row3(ffn2_norm)
    bg, gs, gr, dsk = row3(s5_b_glu), row3(s5_out_norm), row3(ret_out_norm), row3(s5_d)
    gf = final_norm.astype(F32)[None, :]

    h = x.reshape(seq, d)
    for l in range(depth):
        h = _ffn(h, g1, ffn_a, l, t, gf)
        z = _inproj(h, gm, w_in_b, l, t)
        y_ret = _retention(z, cos, sin, ret_tabs, gr, l, rw, t)
        y_ssm = _s5(z, wb, lam, wcr, wci, dsk, w_glu_b, bg, gs, l, 4 * rw, sw, t)
        h = _outproj(h, y_ret, y_ssm, w_out_b, l, t)
        h = _ffn(h, g2, ffn_b, l, t, gf, final_norm=(l == depth - 1))
    return h.reshape(batch, seq, d)
```
